```python
import jax, jax.numpy as jnp
from jax import lax
import numpy as np

D_MODEL = 1024
BATCH = 4
SEQ = 8192
DEPTH = 4

WINDOWS = (128, 512, 2048)
DILATIONS = (1, 4, 16)
N_GROUPS = 3
HEADS_PER_GROUP = 8
HEAD_DIM = 128
N_ATTN_HEADS = N_GROUPS * HEADS_PER_GROUP
ATTN_OUT_WIDTH = HEADS_PER_GROUP * HEAD_DIM
QKV_WIDTH = N_GROUPS * 3 * HEADS_PER_GROUP * HEAD_DIM
NUM_BUCKETS = 32
MAX_DISTANCE = 2048
LRU_WIDTH = D_MODEL
LRU_BLOCKS = 4
LRU_BLOCK_WIDTH = LRU_WIDTH // LRU_BLOCKS
LRU_CONV_WIDTH = 4
LRU_C = 8.0
D_FF = 3 * D_MODEL
FFN_CONV_WIDTH = 3

RMS_EPS = 1e-6
NEG_INF = -1e30
N_ATTN_LAYERS = (DEPTH + 1) // 2
N_LRU_LAYERS = DEPTH // 2

kernel_name = "hybrid_dilated_attn_rglru_convffn"


def rms_norm(x, g):
    xf = x.astype(jnp.float32)
    y = xf * lax.rsqrt(jnp.mean(xf * xf, axis=-1, keepdims=True) + RMS_EPS)
    return (y * g.astype(jnp.float32)).astype(x.dtype)


def causal_dwconv(x, w, b):
    K = w.shape[0]
    S = x.shape[1]
    xp = jnp.pad(x, ((0, 0), (K - 1, 0), (0, 0)))
    out = b
    for k in range(K):
        out = out + xp[:, k:k + S] * w[k]
    return out


def t5_bucket(dist):
    max_exact = NUM_BUCKETS // 2
    d = np.maximum(dist, 1).astype(np.float64)
    large = max_exact + (np.log(d / max_exact) / np.log(MAX_DISTANCE / max_exact)
                         * (NUM_BUCKETS - max_exact)).astype(np.int32)
    large = np.minimum(large, NUM_BUCKETS - 1)
    return np.where(dist < max_exact, dist, large).astype(np.int32)


def band_geometry(band):
    i = np.arange(band)[:, None]
    k = np.arange(2 * band)[None, :]
    m = i + band - k
    valid = (m >= 0) & (m <= band)
    return m, valid, k


def dilated_group_attention(q, k, v, bias_table, window, dilation):
    B, S, H, Dh = q.shape
    band = window // dilation
    span = dilation * band
    S_pad = -(-S // span) * span
    U = S_pad // dilation
    nb = U // band

    def to_strided(t):
        t = jnp.pad(t, ((0, 0), (0, S_pad - S), (0, 0), (0, 0)))
        t = t.reshape(B, U, dilation, H, Dh).transpose(0, 2, 3, 1, 4)
        return t.reshape(B, dilation, H, nb, band, Dh)

    qs, ks, vs = to_strided(q), to_strided(k), to_strided(v)
    pad_blk = ((0, 0), (0, 0), (0, 0), (1, 0), (0, 0), (0, 0))
    kk = jnp.concatenate([jnp.pad(ks, pad_blk)[:, :, :, :-1], ks], axis=-2)
    vv = jnp.concatenate([jnp.pad(vs, pad_blk)[:, :, :, :-1], vs], axis=-2)

    m, valid, kidx = band_geometry(band)
    bucket = t5_bucket(np.clip(m, 0, band) * dilation)
    bias = jnp.transpose(bias_table.astype(jnp.float32)[bucket], (2, 0, 1))
    blk = np.arange(nb)[:, None, None]
    mask = valid[None] & ((blk > 0) | (kidx[None] >= band))

    scale = HEAD_DIM ** -0.5
    logits = jnp.einsum('bxhnqd,bxhnkd->bxhnqk', qs, kk,
                        preferred_element_type=jnp.float32) * scale
    logits = logits + bias[None, None, :, None]
    logits = jnp.where(mask[None, None, None], logits, NEG_INF)
    mx = jnp.max(logits, axis=-1, keepdims=True)
    p = jnp.exp(logits - mx)
    s = jnp.sum(p, axis=-1, keepdims=True)
    o = jnp.einsum('bxhnqk,bxhnkd->bxhnqd', p, vv.astype(jnp.float32)) / s
    lse = (mx + jnp.log(s))[..., 0]

    o = o.reshape(B, dilation, H, U, Dh).transpose(0, 3, 1, 2, 4).reshape(B, S_pad, H, Dh)[:, :S]
    lse = lse.reshape(B, dilation, H, U).transpose(0, 3, 1, 2).reshape(B, S_pad, H)[:, :S]
    return o, lse


def dilated_attention_mixer(x, w_qkv, w_o, rel_bias):
    B, S, _ = x.shape
    qkv = (x @ w_qkv).reshape(B, S, N_GROUPS, 3, HEADS_PER_GROUP, HEAD_DIM)
    outs, lses = [], []
    for g in range(N_GROUPS):
        tbl = rel_bias[:, g * HEADS_PER_GROUP:(g + 1) * HEADS_PER_GROUP]
        o, lse = dilated_group_attention(qkv[:, :, g, 0], qkv[:, :, g, 1], qkv[:, :, g, 2],
                                         tbl, WINDOWS[g], DILATIONS[g])
        outs.append(o)
        lses.append(lse)
    alpha = jax.nn.softmax(jnp.stack(lses), axis=0)
    o = jnp.sum(alpha[..., None] * jnp.stack(outs), axis=0)
    return o.reshape(B, S, ATTN_OUT_WIDTH).astype(x.dtype) @ w_o


def _lru_combine(c1, c2):
    a1, b1 = c1
    a2, b2 = c2
    return a1 * a2, a2 * b1 + b2


def rglru_mixer(x, w_in, conv_w, conv_b, ga_w, ga_b, gx_w, gx_b, lam, w_out):
    B, S, _ = x.shape
    h = x @ w_in
    xb, gb = h[..., :LRU_WIDTH], h[..., LRU_WIDTH:]
    gate = jax.nn.gelu(gb, approximate=True)
    xb = causal_dwconv(xb, conv_w, conv_b)
    xr = xb.reshape(B, S, LRU_BLOCKS, LRU_BLOCK_WIDTH)
    r = jax.nn.sigmoid(jnp.einsum('bsnc,ncd->bsnd', xr, ga_w) + ga_b).reshape(B, S, LRU_WIDTH)
    i = jax.nn.sigmoid(jnp.einsum('bsnc,ncd->bsnd', xr, gx_w) + gx_b).reshape(B, S, LRU_WIDTH)
    log_a = LRU_C * r.astype(jnp.float32) * jax.nn.log_sigmoid(lam.astype(jnp.float32))
    a = jnp.exp(log_a)
    mult = jnp.sqrt(jnp.maximum(-jnp.expm1(2.0 * log_a), 0.0))
    b = mult * i.astype(jnp.float32) * xb.astype(jnp.float32)
    _, hs = lax.associative_scan(_lru_combine, (a, b), axis=1)
    return (hs.astype(x.dtype) * gate) @ w_out


def conv_ffn(x, w_up, conv_w, conv_b, w_down):
    u = causal_dwconv(x @ w_up, conv_w, conv_b)
    g, v = u[..., :D_FF], u[..., D_FF:]
    return (jax.nn.gelu(g, approximate=True) * v) @ w_down


def setup_inputs(seed: int = 0) -> dict:
    key = jax.random.key(seed)
    ks = jax.random.split(key, 32)
    f32 = jnp.float32
    nrm = lambda k, shape, s: jax.random.normal(k, shape, f32) * s
    gain = lambda k: 1.0 + 0.05 * jax.random.normal(k, (DEPTH, D_MODEL), f32)
    a_target = jax.random.uniform(ks[20], (N_LRU_LAYERS, LRU_WIDTH), f32, 0.9, 0.999)
    p = a_target ** (1.0 / LRU_C)
    lam = jnp.log(p) - jnp.log1p(-p)
    return {
        "x": nrm(ks[0], (BATCH, SEQ, D_MODEL), 1.0),
        "norm_mix_pre": gain(ks[1]),
        "norm_mix_post": gain(ks[2]),
        "norm_ffn_pre": gain(ks[3]),
        "norm_ffn_post": gain(ks[4]),
        "rel_bias": nrm(ks[5], (NUM_BUCKETS, N_ATTN_HEADS), 0.5),
        "attn_w_qkv": nrm(ks[6], (N_ATTN_LAYERS, D_MODEL, QKV_WIDTH), D_MODEL ** -0.5),
        "attn_w_o": nrm(ks[7], (N_ATTN_LAYERS, ATTN_OUT_WIDTH, D_MODEL), ATTN_OUT_WIDTH ** -0.5),
        "lru_w_in": nrm(ks[8], (N_LRU_LAYERS, D_MODEL, 2 * LRU_WIDTH), D_MODEL ** -0.5),
        "lru_conv_w": nrm(ks[9], (N_LRU_LAYERS, LRU_CONV_WIDTH, LRU_WIDTH), LRU_CONV_WIDTH ** -0.5),
        "lru_conv_b": nrm(ks[10], (N_LRU_LAYERS, LRU_WIDTH), 0.02),
        "lru_ga_w": nrm(ks[11], (N_LRU_LAYERS, LRU_BLOCKS, LRU_BLOCK_WIDTH, LRU_BLOCK_WIDTH), LRU_BLOCK_WIDTH ** -0.5),
        "lru_ga_b": nrm(ks[12], (N_LRU_LAYERS, LRU_BLOCKS, LRU_BLOCK_WIDTH), 0.02),
        "lru_gx_w": nrm(ks[13], (N_LRU_LAYERS, LRU_BLOCKS, LRU_BLOCK_WIDTH, LRU_BLOCK_WIDTH), LRU_BLOCK_WIDTH ** -0.5),
        "lru_gx_b": nrm(ks[14], (N_LRU_LAYERS, LRU_BLOCKS, LRU_BLOCK_WIDTH), 0.02),
        "lru_lambda": lam,
        "lru_w_out": nrm(ks[15], (N_LRU_LAYERS, LRU_WIDTH, D_MODEL), LRU_WIDTH ** -0.5),
        "ffn_w_up": nrm(ks[16], (DEPTH, D_MODEL, 2 * D_FF), D_MODEL ** -0.5),
        "ffn_conv_w": nrm(ks[17], (DEPTH, FFN_CONV_WIDTH, 2 * D_FF), FFN_CONV_WIDTH ** -0.5),
        "ffn_conv_b": nrm(ks[18], (DEPTH, 2 * D_FF), 0.02),
        "ffn_w_down": nrm(ks[19], (DEPTH, D_FF, D_MODEL), D_FF ** -0.5),
    }


def reference(x, norm_mix_pre, norm_mix_post, norm_ffn_pre, norm_ffn_post, rel_bias,
              attn_w_qkv, attn_w_o, lru_w_in, lru_conv_w, lru_conv_b, lru_ga_w, lru_ga_b,
              lru_gx_w, lru_gx_b, lru_lambda, lru_w_out, ffn_w_up, ffn_conv_w, ffn_conv_b,
              ffn_w_down):
    for layer in range(DEPTH):
        j = layer // 2
        h = rms_norm(x, norm_mix_pre[layer])
        if layer % 2 == 0:
            h = dilated_attention_mixer(h, attn_w_qkv[j], attn_w_o[j], rel_bias)
        else:
            h = rglru_mixer(h, lru_w_in[j], lru_conv_w[j], lru_conv_b[j], lru_ga_w[j], lru_ga_b[j],
                            lru_gx_w[j], lru_gx_b[j], lru_lambda[j], lru_w_out[j])
        x = x + rms_norm(h, norm_mix_post[layer])
        h = rms_norm(x, norm_ffn_pre[layer])
        h = conv_ffn(h, ffn_w_up[layer], ffn_conv_w[layer], ffn_conv_b[layer], ffn_w_down[layer])
        x = x + rms_norm(h, norm_ffn_post[layer])
    return x
```

```python
import functools

import numpy as np
import jax
import jax.numpy as jnp
from jax import lax
from jax.experimental import pallas as pl
from jax.experimental.pallas import tpu as pltpu

F32 = jnp.float32
BF16 = jnp.bfloat16

RMS_EPS = 1e-6
NEG_INF = -1e30
HEAD_DIM = 128
HEADS_PER_GROUP = 8
WINDOWS = (128, 512, 2048)
DILATIONS = (1, 4, 16)
N_GROUPS = 3
BAND = 128
NUM_BUCKETS = 32
MAX_DISTANCE = 2048
LRU_C = 8.0
LRU_BLOCKS = 4
SUBLANES = 8
LANES = 128
VMEM_LIMIT_BYTES = 56 * 1024 * 1024
ROW_TILE = 512


def _params(*semantics):
    return pltpu.CompilerParams(dimension_semantics=semantics,
                                vmem_limit_bytes=VMEM_LIMIT_BYTES)


def _resident(shape):
    nd = len(shape)
    return pl.BlockSpec(shape, lambda *_: (0,) * nd, pipeline_mode=pl.Buffered(1))


def _rms(x, g):
    return x * lax.rsqrt(jnp.mean(x * x, axis=-1, keepdims=True) + RMS_EPS) * g


def _dot(a, b):
    return jnp.dot(a, b, preferred_element_type=F32)


def _sigmoid(x):
    return 1.0 / (1.0 + jnp.exp(-x))


def _t5_bucket(dist):
    max_exact = NUM_BUCKETS // 2
    d = np.maximum(dist, 1).astype(np.float64)
    large = max_exact + (np.log(d / max_exact) / np.log(MAX_DISTANCE / max_exact)
                         * (NUM_BUCKETS - max_exact)).astype(np.int32)
    large = np.minimum(large, NUM_BUCKETS - 1)
    return np.where(dist < max_exact, dist, large).astype(np.int32)


def _bucket_tables():
    i = np.arange(BAND)[:, None]
    k = np.arange(2 * BAND)[None, :]
    m = i + BAND - k
    valid = (m >= 0) & (m <= BAND)
    tabs = [np.where(valid, _t5_bucket(np.clip(m, 0, BAND) * d), -1) for d in DILATIONS]
    return np.stack(tabs).astype(np.int32)


def _bias_kernel(tbl_ref, bucket_ref, out_ref):
    h = pl.program_id(0)
    bucket = bucket_ref[0]
    acc = jnp.where(bucket < 0, NEG_INF, 0.0).astype(F32)
    for j in range(NUM_BUCKETS):
        acc = jnp.where(bucket == j, tbl_ref[j, h], acc)
    out_ref[0] = acc


def _bias_tables(rel_bias):
    n_heads = rel_bias.shape[1]
    buckets = jnp.asarray(_bucket_tables())
    return pl.pallas_call(
        _bias_kernel,
        grid=(n_heads,),
        in_specs=[pl.BlockSpec(memory_space=pltpu.SMEM),
                  pl.BlockSpec((1, BAND, 2 * BAND), lambda h: (h // HEADS_PER_GROUP, 0, 0))],
        out_specs=pl.BlockSpec((1, BAND, 2 * BAND), lambda h: (h, 0, 0)),
        out_shape=jax.ShapeDtypeStruct((n_heads, BAND, 2 * BAND), F32),
        compiler_params=_params("arbitrary"),
        name="bias_tables",
    )(rel_bias.astype(F32), buckets)


def _qkv_kernel(x_ref, g_ref, w_ref, o_ref, *, col_chunk):
    h = _rms(x_ref[...], g_ref[...]).astype(BF16)
    for c in range(0, w_ref.shape[1], col_chunk):
        o_ref[:, c:c + col_chunk] = _dot(h, w_ref[:, c:c + col_chunk]).astype(o_ref.dtype)


def _qkv_proj(x2, g, w):
    m, d = x2.shape
    n = w.shape[1]
    tm = min(ROW_TILE, m)
    return pl.pallas_call(
        functools.partial(_qkv_kernel, col_chunk=1024),
        grid=(m // tm,),
        in_specs=[pl.BlockSpec((tm, d), lambda i: (i, 0)), _resident((1, d)), _resident((d, n))],
        out_specs=pl.BlockSpec((tm, n), lambda i: (i, 0)),
        out_shape=jax.ShapeDtypeStruct((m, n), BF16),
        compiler_params=_params("parallel"),
        name="qkv_proj",
    )(x2, g.reshape(1, d), w)


def _attn_kernel(q_ref, k_ref, v_ref, kp_ref, vp_ref, bias_ref, o_ref, lse_ref, kk_ref, vv_ref):
    tq = q_ref.shape[1]
    first_tile = pl.program_id(2) == 0
    kk_ref[0:BAND, :] = kp_ref[0]
    kk_ref[BAND:, :] = k_ref[0]
    vv_ref[0:BAND, :] = vp_ref[0]
    vv_ref[BAND:, :] = v_ref[0]
    scale = HEAD_DIM ** -0.5
    lane = lax.broadcasted_iota(jnp.int32, (BAND, LANES), 1)
    key_col = lax.broadcasted_iota(jnp.int32, (BAND, 2 * BAND), 1)
    for j in range(tq // BAND):
        rows = slice(j * BAND, (j + 1) * BAND)
        win = slice(j * BAND, (j + 2) * BAND)
        lse_tile = jnp.zeros((BAND, LANES), F32)
        for h in range(HEADS_PER_GROUP):
            cols = slice(h * HEAD_DIM, (h + 1) * HEAD_DIM)
            logits = lax.dot_general(q_ref[0, rows, cols], kk_ref[win, cols],
                                     (((1,), (1,)), ((), ())), preferred_element_type=F32)
            logits = logits * scale + bias_ref[h]
            if j == 0:
                logits = jnp.where(first_tile & (key_col < BAND), NEG_INF, logits)
            mx = jnp.max(logits, axis=-1, keepdims=True)
            p = jnp.exp(logits - mx)
            s = jnp.sum(p, axis=-1, keepdims=True)
            o = _dot(p.astype(BF16), vv_ref[win, cols]) / s
            o_ref[0, rows, cols] = o.astype(o_ref.dtype)
            lse_tile = jnp.where(lane == h, mx + jnp.log(s), lse_tile)
        lse_ref[0, rows, :] = lse_tile


def _group_attention(qkv, bias, group, batch, seq):
    d = DILATIONS[group]
    width = HEADS_PER_GROUP * HEAD_DIM
    n_slabs = qkv.shape[1] // width
    u = seq // d
    tq = min(ROW_TILE, u)
    nblk = tq // BAND
    view = qkv.reshape(batch, u, d * qkv.shape[1])

    def slab(which):
        return lambda b, r, i: (b, i, r * n_slabs + 3 * group + which)

    def prev_slab(which):
        return lambda b, r, i: (b, jnp.maximum(i * nblk - 1, 0), r * n_slabs + 3 * group + which)

    o, lse = pl.pallas_call(
        _attn_kernel,
        grid=(batch, d, u // tq),
        in_specs=[pl.BlockSpec((1, tq, width), slab(0)),
                  pl.BlockSpec((1, tq, width), slab(1)),
                  pl.BlockSpec((1, tq, width), slab(2)),
                  pl.BlockSpec((1, BAND, width), prev_slab(1)),
                  pl.BlockSpec((1, BAND, width), prev_slab(2)),
                  pl.BlockSpec((HEADS_PER_GROUP, BAND, 2 * BAND),
                               lambda b, r, i: (group, 0, 0))],
        out_specs=[pl.BlockSpec((1, tq, width), lambda b, r, i: (b, i, r)),
                   pl.BlockSpec((1, tq, LANES), lambda b, r, i: (b, i, r))],
        out_shape=[jax.ShapeDtypeStruct((batch, u, d * width), BF16),
                   jax.ShapeDtypeStruct((batch, u, d * LANES), F32)],
        scratch_shapes=[pltpu.VMEM((BAND + tq, width), BF16),
                        pltpu.VMEM((BAND + tq, width), BF16)],
        compiler_params=_params("parallel", "parallel", "parallel"),
        name=f"attn_group{group}",
    )(view, view, view, view, view, bias)
    return o.reshape(batch * seq, width), lse.reshape(batch * seq, LANES)


def _combine_kernel(x_ref, o0_ref, o1_ref, o2_ref, l0_ref, l1_ref, l2_ref, w_ref, g_ref, out_ref):
    lses = [l0_ref[...], l1_ref[...], l2_ref[...]]
    outs = [o0_ref, o1_ref, o2_ref]
    mx = jnp.maximum(jnp.maximum(lses[0], lses[1]), lses[2])
    es = [jnp.exp(l - mx) for l in lses]
    inv = 1.0 / (es[0] + es[1] + es[2])
    alphas = [e * inv for e in es]
    parts = []
    for h in range(HEADS_PER_GROUP):
        cols = slice(h * HEAD_DIM, (h + 1) * HEAD_DIM)
        acc = alphas[0][:, h:h + 1] * outs[0][:, cols].astype(F32)
        for g in range(1, N_GROUPS):
            acc = acc + alphas[g][:, h:h + 1] * outs[g][:, cols].astype(F32)
        parts.append(acc.astype(BF16))
    o = jnp.concatenate(parts, axis=1)
    out_ref[...] = x_ref[...] + _rms(_dot(o, w_ref[...]), g_ref[...])


def _combine_project(x2, outs, lses, w_o, g_post):
    m, d = x2.shape
    tm = min(ROW_TILE, m)
    row = lambda width: pl.BlockSpec((tm, width), lambda i: (i, 0))
    return pl.pallas_call(
        _combine_kernel,
        grid=(m // tm,),
        in_specs=[row(d)] + [row(d)] * N_GROUPS + [row(LANES)] * N_GROUPS
                 + [_resident(w_o.shape), _resident((1, d))],
        out_specs=row(d),
        out_shape=jax.ShapeDtypeStruct((m, d), F32),
        compiler_params=_params("parallel"),
        name="attn_combine_out",
    )(x2, *outs, *lses, w_o, g_post.reshape(1, d))


def _attention_layer(x, g_pre, g_post, w_qkv, w_o, bias):
    batch, seq, d = x.shape
    x2 = x.reshape(batch * seq, d)
    qkv = _qkv_proj(x2, g_pre, w_qkv.astype(BF16))
    outs, lses = [], []
    for group in range(N_GROUPS):
        o, lse = _group_attention(qkv, bias, group, batch, seq)
        outs.append(o)
        lses.append(lse)
    y = _combine_project(x2, outs, lses, w_o.astype(BF16), g_post)
    return y.reshape(batch, seq, d)


def _log_sigmoid(x):
    return jnp.minimum(x, 0.0) - jnp.log1p(jnp.exp(-jnp.abs(x)))


def _lru_kernel(x_ref, gpre_ref, gpost_ref, win_ref, cw_ref, cb_ref, gaw_ref, gab_ref,
                gxw_ref, gxb_ref, lam_ref, wout_ref, out_ref, xs_ref, a_ref, b_ref, h_ref):
    tm, width = a_ref.shape
    taps = cw_ref.shape[0]

    @pl.when(pl.program_id(1) == 0)
    def _():
        xs_ref[0:SUBLANES, :] = jnp.zeros((SUBLANES, width), F32)
        h_ref[...] = jnp.zeros_like(h_ref)

    x = x_ref[0]
    hn = _rms(x, gpre_ref[...]).astype(BF16)
    xs_ref[SUBLANES:, :] = _dot(hn, win_ref[:, :width])
    gate = jax.nn.gelu(_dot(hn, win_ref[:, width:]), approximate=True)

    xc = cb_ref[...]
    for k in range(taps):
        start = SUBLANES - (taps - 1) + k
        xc = xc + xs_ref[start:start + tm, :] * cw_ref[k:k + 1, :]
    xs_ref[0:SUBLANES, :] = xs_ref[tm:tm + SUBLANES, :]

    xcb = xc.astype(BF16)
    blk = width // LRU_BLOCKS
    r_parts, i_parts = [], []
    for n in range(LRU_BLOCKS):
        xb = xcb[:, n * blk:(n + 1) * blk]
        r_parts.append(_sigmoid(_dot(xb, gaw_ref[n]) + gab_ref[n:n + 1, :]))
        i_parts.append(_sigmoid(_dot(xb, gxw_ref[n]) + gxb_ref[n:n + 1, :]))
    r = jnp.concatenate(r_parts, axis=1)
    gi = jnp.concatenate(i_parts, axis=1)
    log_a = LRU_C * r * _log_sigmoid(lam_ref[...])
    a_ref[...] = jnp.exp(log_a)
    mult = jnp.sqrt(jnp.maximum(1.0 - jnp.exp(2.0 * log_a), 0.0))
    b_ref[...] = mult * gi * xc

    row = lax.broadcasted_iota(jnp.int32, (SUBLANES, width), 0)

    def group_scan(v, h):
        rows = pl.ds(pl.multiple_of(v * SUBLANES, SUBLANES), SUBLANES)
        a = a_ref[rows, :]
        b = b_ref[rows, :]
        for s in (1, 2, 4):
            a_prev = jnp.where(row >= s, pltpu.roll(a, s, 0), 1.0)
            b_prev = jnp.where(row >= s, pltpu.roll(b, s, 0), 0.0)
            b = a * b_prev + b
            a = a * a_prev
        hs = a * h + b
        b_ref[rows, :] = hs
        return hs[SUBLANES - 1:SUBLANES, :]

    h_ref[...] = lax.fori_loop(0, tm // SUBLANES, group_scan, h_ref[...], unroll=4)

    y = _dot((b_ref[...] * gate).astype(BF16), wout_ref[...])
    out_ref[0] = x + _rms(y, gpost_ref[...])


def _lru_layer(x, g_pre, g_post, w_in, conv_w, conv_b, ga_w, ga_b, gx_w, gx_b, lam, w_out):
    batch, seq, d = x.shape
    width = w_out.shape[0]
    tm = min(ROW_TILE, seq)
    tile = pl.BlockSpec((1, tm, d), lambda b, i: (b, i, 0))
    ops = [g_pre.reshape(1, d), g_post.reshape(1, d), w_in.astype(BF16), conv_w,
           conv_b.reshape(1, width), ga_w.astype(BF16), ga_b, gx_w.astype(BF16), gx_b,
           lam.reshape(1, width), w_out.astype(BF16)]
    return pl.pallas_call(
        _lru_kernel,
        grid=(batch, seq // tm),
        in_specs=[tile] + [_resident(o.shape) for o in ops],
        out_specs=tile,
        out_shape=jax.ShapeDtypeStruct(x.shape, F32),
        scratch_shapes=[pltpu.VMEM((SUBLANES + tm, width), F32),
                        pltpu.VMEM((tm, width), F32),
                        pltpu.VMEM((tm, width), F32),
                        pltpu.VMEM((1, width), F32)],
        compiler_params=_params("parallel", "arbitrary"),
        name="rglru_layer",
    )(x, *ops)


def _ffn_kernel(x_ref, gpre_ref, gpost_ref, wup_ref, cw_ref, cb_ref, wdn_ref, out_ref,
                halo_ref, us_ref, acc_ref, *, f_chunk):
    tm = acc_ref.shape[0]
    d_ff = wdn_ref.shape[0]
    taps = cw_ref.shape[0]

    @pl.when(pl.program_id(1) == 0)
    def _():
        halo_ref[...] = jnp.zeros_like(halo_ref)

    x = x_ref[0]
    hn = _rms(x, gpre_ref[...]).astype(BF16)
    for c in range(d_ff // f_chunk):
        halves = []
        for part, off in enumerate((c * f_chunk, d_ff + c * f_chunk)):
            src = slice(off, off + f_chunk)
            dst = slice(part * f_chunk, (part + 1) * f_chunk)
            us_ref[0:SUBLANES, dst] = halo_ref[:, src]
            us_ref[SUBLANES:, dst] = _dot(hn, wup_ref[:, src])
            u = cb_ref[:, src]
            for k in range(taps):
                start = SUBLANES - (taps - 1) + k
                u = u + us_ref[start:start + tm, dst] * cw_ref[k:k + 1, src]
            halo_ref[:, src] = us_ref[tm:tm + SUBLANES, dst]
            halves.append(u)
        act = (jax.nn.gelu(halves[0], approximate=True) * halves[1]).astype(BF16)
        contrib = _dot(act, wdn_ref[c * f_chunk:(c + 1) * f_chunk, :])
        if c == 0:
            acc_ref[...] = contrib
        else:
            acc_ref[...] += contrib
    out_ref[0] = x + _rms(acc_ref[...], gpost_ref[...])


def _ffn_layer(x, g_pre, g_post, w_up, conv_w, conv_b, w_down):
    batch, seq, d = x.shape
    d_ff = w_down.shape[0]
    f_chunk = 512
    tm = min(ROW_TILE, seq)
    tile = pl.BlockSpec((1, tm, d), lambda b, i: (b, i, 0))
    ops = [g_pre.reshape(1, d), g_post.reshape(1, d), w_up.astype(BF16), conv_w,
           conv_b.reshape(1, 2 * d_ff), w_down.astype(BF16)]
    return pl.pallas_call(
        functools.partial(_ffn_kernel, f_chunk=f_chunk),
        grid=(batch, seq // tm),
        in_specs=[tile] + [_resident(o.shape) for o in ops],
        out_specs=tile,
        out_shape=jax.ShapeDtypeStruct(x.shape, F32),
        scratch_shapes=[pltpu.VMEM((SUBLANES, 2 * d_ff), F32),
                        pltpu.VMEM((SUBLANES + tm, 2 * f_chunk), F32),
                        pltpu.VMEM((tm, d), F32)],
        compiler_params=_params("parallel", "arbitrary"),
        name="conv_ffn",
    )(x, *ops)


def kernel(x, norm_mix_pre, norm_mix_post, norm_ffn_pre, norm_ffn_post, rel_bias, attn_w_qkv, attn_w_o, lru_w_in, lru_conv_w, lru_conv_b, lru_ga_w, lru_ga_b, lru_gx_w, lru_gx_b, lru_lambda, lru_w_out, ffn_w_up, ffn_conv_w, ffn_conv_b, ffn_w_down):
    bias = _bias_tables(rel_bias)
    for layer in range(norm_mix_pre.shape[0]):
        j = layer // 2
        if layer % 2 == 0:
            x = _attention_layer(x, norm_mix_pre[layer], norm_mix_post[layer],
                                 attn_w_qkv[j], attn_w_o[j], bias)
        else:
            x = _lru_layer(x, norm_mix_pre[layer], norm_mix_post[layer], lru_w_in[j],
                           lru_conv_w[j], lru_conv_b[j], lru_ga_w[j], lru_ga_b[j],
                           lru_gx_w[j], lru_gx_b[j], lru_lambda[j], lru_w_out[j])
        x = _ffn_layer(x, norm_ffn_pre[layer], norm_ffn_post[layer], ffn_w_up[layer],
                       ffn_conv_w[layer], ffn_conv_b[layer], ffn_w_down[layer])
    return x
```

```python
import functools

import numpy as np
import jax
import jax.numpy as jnp
from jax import lax
from jax.experimental import pallas as pl
from jax.experimental.pallas import tpu as pltpu

F32 = jnp.float32
BF16 = jnp.bfloat16

RMS_EPS = 1e-6
NEG_INF = -1e30
HEAD_DIM = 128
HEADS_PER_GROUP = 8
WINDOWS = (128, 512, 2048)
DILATIONS = (1, 4, 16)
N_GROUPS = 3
BAND = 128
NUM_BUCKETS = 32
MAX_DISTANCE = 2048
LRU_C = 8.0
LRU_BLOCKS = 4
SUBLANES = 8
LANES = 128
VMEM_LIMIT_BYTES = 56 * 1024 * 1024
ROW_TILE = 512


def _params(*semantics):
    return pltpu.CompilerParams(dimension_semantics=semantics,
                                vmem_limit_bytes=VMEM_LIMIT_BYTES)


def _resident(shape):
    nd = len(shape)
    return pl.BlockSpec(shape, lambda *_: (0,) * nd, pipeline_mode=pl.Buffered(1))


def _rms(x, g):
    return x * lax.rsqrt(jnp.mean(x * x, axis=-1, keepdims=True) + RMS_EPS) * g


def _dot(a, b):
    return jnp.dot(a, b, preferred_element_type=F32)


def _gelu_tanh(x):
    c = float(np.sqrt(2.0 / np.pi))
    t = jnp.tanh(x * (c + (c * 0.044715) * (x * x)))
    return x * (0.5 + 0.5 * t)


def _sigmoid(x):
    return 1.0 / (1.0 + jnp.exp(-x))


def _t5_bucket(dist):
    max_exact = NUM_BUCKETS // 2
    d = np.maximum(dist, 1).astype(np.float64)
    large = max_exact + (np.log(d / max_exact) / np.log(MAX_DISTANCE / max_exact)
                         * (NUM_BUCKETS - max_exact)).astype(np.int32)
    large = np.minimum(large, NUM_BUCKETS - 1)
    return np.where(dist < max_exact, dist, large).astype(np.int32)


def _bucket_tables():
    i = np.arange(BAND)[:, None]
    k = np.arange(2 * BAND)[None, :]
    m = i + BAND - k
    valid = (m >= 0) & (m <= BAND)
    tabs = [np.where(valid, _t5_bucket(np.clip(m, 0, BAND) * d), -1) for d in DILATIONS]
    return np.stack(tabs).astype(np.int32)


def _bias_kernel(tbl_ref, bucket_ref, out_ref):
    h = pl.program_id(0)
    bucket = bucket_ref[0]
    acc = jnp.where(bucket < 0, NEG_INF, 0.0).astype(F32)
    for j in range(NUM_BUCKETS):
        acc = jnp.where(bucket == j, tbl_ref[j, h], acc)
    out_ref[0] = acc


def _bias_tables(rel_bias):
    n_heads = rel_bias.shape[1]
    buckets = jnp.asarray(_bucket_tables())
    return pl.pallas_call(
        _bias_kernel,
        grid=(n_heads,),
        in_specs=[pl.BlockSpec(memory_space=pltpu.SMEM),
                  pl.BlockSpec((1, BAND, 2 * BAND), lambda h: (h // HEADS_PER_GROUP, 0, 0))],
        out_specs=pl.BlockSpec((1, BAND, 2 * BAND), lambda h: (h, 0, 0)),
        out_shape=jax.ShapeDtypeStruct((n_heads, BAND, 2 * BAND), F32),
        compiler_params=_params("arbitrary"),
        name="bias_tables",
    )(rel_bias.astype(F32), buckets)


GROUP_WIDTH = HEADS_PER_GROUP * HEAD_DIM
N_SLABS = GROUP_WIDTH // LANES


def _qkv_kernel(x_ref, g_ref, w_ref, o0_ref, o1_ref, o2_ref, hs_ref, hp_ref):
    tm = x_ref.shape[1]
    h = _rms(x_ref[0], g_ref[...])
    hb = h.astype(BF16)
    for c in range(3):
        cols = slice(c * GROUP_WIDTH, (c + 1) * GROUP_WIDTH)
        o0_ref[0, 0, :, cols] = _dot(hb, w_ref[:, cols]).astype(o0_ref.dtype)
    for j in range(N_SLABS):
        hs_ref[j] = h[:, j * LANES:(j + 1) * LANES]
    for group, o_ref in ((1, o1_ref), (2, o2_ref)):
        d = DILATIONS[group]
        n = tm // d
        for r in range(d):
            for j in range(N_SLABS):
                hp_ref[r * n:(r + 1) * n, j * LANES:(j + 1) * LANES] = (
                    hs_ref[j, pl.ds(r, n, stride=d), :].astype(BF16))
        hp = hp_ref[...]
        for c in range(3):
            cols = slice(c * GROUP_WIDTH, (c + 1) * GROUP_WIDTH)
            wcols = slice((3 * group + c) * GROUP_WIDTH, (3 * group + c + 1) * GROUP_WIDTH)
            res = _dot(hp, w_ref[:, wcols]).astype(o_ref.dtype)
            for r in range(d):
                o_ref[0, r, :, cols] = res[r * n:(r + 1) * n, :]


def _qkv_proj(x, g, w):
    batch, seq, d_model = x.shape
    tm = min(ROW_TILE, seq)
    out_specs, out_shape = [], []
    for d in DILATIONS:
        out_specs.append(pl.BlockSpec((1, d, tm // d, 3 * GROUP_WIDTH), lambda b, i: (b, 0, i, 0)))
        out_shape.append(jax.ShapeDtypeStruct((batch, d, seq // d, 3 * GROUP_WIDTH), BF16))
    return pl.pallas_call(
        _qkv_kernel,
        grid=(batch, seq // tm),
        in_specs=[pl.BlockSpec((1, tm, d_model), lambda b, i: (b, i, 0)),
                  _resident((1, d_model)), _resident(w.shape)],
        out_specs=out_specs,
        out_shape=out_shape,
        scratch_shapes=[pltpu.VMEM((N_SLABS, tm, LANES), F32),
                        pltpu.VMEM((tm, d_model), BF16)],
        compiler_params=_params("parallel", "parallel"),
        name="qkv_proj",
    )(x, g.reshape(1, d_model), w)


def _attn_kernel(q_ref, k_ref, v_ref, kp_ref, vp_ref, bias_ref, o_ref, lse_ref, kk_ref, vv_ref):
    tq = q_ref.shape[2]
    first_tile = pl.program_id(2) == 0
    kk_ref[0:BAND, :] = kp_ref[0, 0]
    kk_ref[BAND:, :] = k_ref[0, 0]
    vv_ref[0:BAND, :] = vp_ref[0, 0]
    vv_ref[BAND:, :] = v_ref[0, 0]
    scale = HEAD_DIM ** -0.5
    lane = lax.broadcasted_iota(jnp.int32, (BAND, LANES), 1)
    key_col = lax.broadcasted_iota(jnp.int32, (BAND, 2 * BAND), 1)
    for j in range(tq // BAND):
        rows = slice(j * BAND, (j + 1) * BAND)
        win = slice(j * BAND, (j + 2) * BAND)
        lse_tile = jnp.zeros((BAND, LANES), F32)
        for h in range(HEADS_PER_GROUP):
            cols = slice(h * HEAD_DIM, (h + 1) * HEAD_DIM)
            logits = lax.dot_general(q_ref[0, 0, rows, cols], kk_ref[win, cols],
                                     (((1,), (1,)), ((), ())), preferred_element_type=F32)
            logits = logits * scale + bias_ref[h]
            if j == 0:
                logits = jnp.where(first_tile & (key_col < BAND), NEG_INF, logits)
            mx = jnp.max(logits, axis=-1, keepdims=True)
            p = jnp.exp(logits - mx)
            s = jnp.sum(p, axis=-1, keepdims=True)
            o = _dot(p.astype(BF16), vv_ref[win, cols]) / s
            o_ref[0, 0, rows, cols] = o.astype(o_ref.dtype)
            lse_tile = jnp.where(lane == h, mx + jnp.log(s), lse_tile)
        lse_ref[0, 0, rows, :] = lse_tile


def _group_attention(qkv, bias, group):
    batch, d, u, _ = qkv.shape
    tq = min(ROW_TILE, u)
    nblk = tq // BAND

    def slab(which):
        return lambda b, r, i: (b, r, i, which)

    def prev_slab(which):
        return lambda b, r, i: (b, r, jnp.maximum(i * nblk - 1, 0), which)

    return pl.pallas_call(
        _attn_kernel,
        grid=(batch, d, u // tq),
        in_specs=[pl.BlockSpec((1, 1, tq, GROUP_WIDTH), slab(0)),
                  pl.BlockSpec((1, 1, tq, GROUP_WIDTH), slab(1)),
                  pl.BlockSpec((1, 1, tq, GROUP_WIDTH), slab(2)),
                  pl.BlockSpec((1, 1, BAND, GROUP_WIDTH), prev_slab(1)),
                  pl.BlockSpec((1, 1, BAND, GROUP_WIDTH), prev_slab(2)),
                  pl.BlockSpec((HEADS_PER_GROUP, BAND, 2 * BAND), lambda b, r, i: (group, 0, 0))],
        out_specs=[pl.BlockSpec((1, 1, tq, GROUP_WIDTH), lambda b, r, i: (b, r, i, 0)),
                   pl.BlockSpec((1, 1, tq, LANES), lambda b, r, i: (b, r, i, 0))],
        out_shape=[jax.ShapeDtypeStruct((batch, d, u, GROUP_WIDTH), BF16),
                   jax.ShapeDtypeStruct((batch, d, u, LANES), F32)],
        scratch_shapes=[pltpu.VMEM((BAND + tq, GROUP_WIDTH), BF16),
                        pltpu.VMEM((BAND + tq, GROUP_WIDTH), BF16)],
        compiler_params=_params("parallel", "parallel", "parallel"),
        name=f"attn_group{group}",
    )(qkv, qkv, qkv, qkv, qkv, bias)


def _combine_kernel(x_ref, o0_ref, o1_ref, o2_ref, l0_ref, l1_ref, l2_ref, w_ref, g_ref,
                    out_ref, lt_ref, ot_ref):
    tm = x_ref.shape[1]
    for gi, (o_ref, l_ref) in enumerate(((o1_ref, l1_ref), (o2_ref, l2_ref))):
        d = DILATIONS[gi + 1]
        n = tm // d
        for r in range(d):
            lt_ref[gi, pl.ds(r, n, stride=d), :] = l_ref[0, r]
            for j in range(N_SLABS):
                ot_ref[gi, j, pl.ds(r, n, stride=d), :] = (
                    o_ref[0, r, :, j * LANES:(j + 1) * LANES].astype(F32))
    lses = [l0_ref[0, 0], lt_ref[0], lt_ref[1]]
    mx = jnp.maximum(jnp.maximum(lses[0], lses[1]), lses[2])
    es = [jnp.exp(l - mx) for l in lses]
    inv = 1.0 / (es[0] + es[1] + es[2])
    alphas = [e * inv for e in es]
    parts = []
    for h in range(HEADS_PER_GROUP):
        cols = slice(h * HEAD_DIM, (h + 1) * HEAD_DIM)
        acc = alphas[0][:, h:h + 1] * o0_ref[0, 0, :, cols].astype(F32)
        for gi in range(N_GROUPS - 1):
            acc = acc + alphas[gi + 1][:, h:h + 1] * ot_ref[gi, h]
        parts.append(acc.astype(BF16))
    o = jnp.concatenate(parts, axis=1)
    out_ref[0] = x_ref[0] + _rms(_dot(o, w_ref[...]), g_ref[...])


def _combine_project(x, outs, lses, w_o, g_post):
    batch, seq, d_model = x.shape
    tm = min(ROW_TILE, seq)
    tile = pl.BlockSpec((1, tm, d_model), lambda b, i: (b, i, 0))
    stream = lambda d, width: pl.BlockSpec((1, d, tm // d, width), lambda b, i: (b, 0, i, 0))
    return pl.pallas_call(
        _combine_kernel,
        grid=(batch, seq // tm),
        in_specs=[tile] + [stream(d, GROUP_WIDTH) for d in DILATIONS]
                 + [stream(d, LANES) for d in DILATIONS]
                 + [_resident(w_o.shape), _resident((1, d_model))],
        out_specs=tile,
        out_shape=jax.ShapeDtypeStruct(x.shape, F32),
        scratch_shapes=[pltpu.VMEM((N_GROUPS - 1, tm, LANES), F32),
                        pltpu.VMEM((N_GROUPS - 1, N_SLABS, tm, LANES), F32)],
        compiler_params=_params("parallel", "parallel"),
        name="attn_combine_out",
    )(x, *outs, *lses, w_o, g_post.reshape(1, d_model))


def _attention_layer(x, g_pre, g_post, w_qkv, w_o, bias):
    qkvs = _qkv_proj(x, g_pre, w_qkv.astype(BF16))
    outs, lses = [], []
    for group in range(N_GROUPS):
        o, lse = _group_attention(qkvs[group], bias, group)
        outs.append(o)
        lses.append(lse)
    return _combine_project(x, outs, lses, w_o.astype(BF16), g_post)


def _log_sigmoid(x):
    return jnp.minimum(x, 0.0) - jnp.log1p(jnp.exp(-jnp.abs(x)))


def _lru_kernel(x_ref, gpre_ref, gpost_ref, win_ref, cw_ref, cb_ref, gaw_ref, gab_ref,
                gxw_ref, gxb_ref, lam_ref, wout_ref, out_ref, xs_ref, a_ref, b_ref, h_ref):
    tm, width = a_ref.shape
    taps = cw_ref.shape[0]

    @pl.when(pl.program_id(1) == 0)
    def _():
        xs_ref[0:SUBLANES, :] = jnp.zeros((SUBLANES, width), F32)
        h_ref[...] = jnp.zeros_like(h_ref)

    x = x_ref[0]
    hn = _rms(x, gpre_ref[...]).astype(BF16)
    xs_ref[SUBLANES:, :] = _dot(hn, win_ref[:, :width])
    gate = _gelu_tanh(_dot(hn, win_ref[:, width:]))

    xc = cb_ref[...]
    for k in range(taps):
        start = SUBLANES - (taps - 1) + k
        xc = xc + xs_ref[start:start + tm, :] * cw_ref[k:k + 1, :]
    xs_ref[0:SUBLANES, :] = xs_ref[tm:tm + SUBLANES, :]

    xcb = xc.astype(BF16)
    blk = width // LRU_BLOCKS
    r_parts, i_parts = [], []
    for n in range(LRU_BLOCKS):
        xb = xcb[:, n * blk:(n + 1) * blk]
        r_parts.append(_sigmoid(_dot(xb, gaw_ref[n]) + gab_ref[n:n + 1, :]))
        i_parts.append(_sigmoid(_dot(xb, gxw_ref[n]) + gxb_ref[n:n + 1, :]))
    r = jnp.concatenate(r_parts, axis=1)
    gi = jnp.concatenate(i_parts, axis=1)
    log_a = LRU_C * r * _log_sigmoid(lam_ref[...])
    a_ref[...] = jnp.exp(log_a)
    mult = jnp.sqrt(jnp.maximum(1.0 - jnp.exp(2.0 * log_a), 0.0))
    b_ref[...] = mult * gi * xc

    row = lax.broadcasted_iota(jnp.int32, (SUBLANES, width), 0)

    def group_scan(v, h):
        rows = pl.ds(pl.multiple_of(v * SUBLANES, SUBLANES), SUBLANES)
        a = a_ref[rows, :]
        b = b_ref[rows, :]
        for s in (1, 2, 4):
            a_prev = jnp.where(row >= s, pltpu.roll(a, s, 0), 1.0)
            b_prev = jnp.where(row >= s, pltpu.roll(b, s, 0), 0.0)
            b = a * b_prev + b
            a = a * a_prev
        hs = a * h + b
        b_ref[rows, :] = hs
        return hs[SUBLANES - 1:SUBLANES, :]

    h_ref[...] = lax.fori_loop(0, tm // SUBLANES, group_scan, h_ref[...], unroll=4)

    y = _dot((b_ref[...] * gate).astype(BF16), wout_ref[...])
    out_ref[0] = x + _rms(y, gpost_ref[...])


def _lru_layer(x, g_pre, g_post, w_in, conv_w, conv_b, ga_w, ga_b, gx_w, gx_b, lam, w_out):
    batch, seq, d = x.shape
    width = w_out.shape[0]
    tm = min(ROW_TILE, seq)
    tile = pl.BlockSpec((1, tm, d), lambda b, i: (b, i, 0))
    ops = [g_pre.reshape(1, d), g_post.reshape(1, d), w_in.astype(BF16), conv_w,
           conv_b.reshape(1, width), ga_w.astype(BF16), ga_b, gx_w.astype(BF16), gx_b,
           lam.reshape(1, width), w_out.astype(BF16)]
    return pl.pallas_call(
        _lru_kernel,
        grid=(batch, seq // tm),
        in_specs=[tile] + [_resident(o.shape) for o in ops],
        out_specs=tile,
        out_shape=jax.ShapeDtypeStruct(x.shape, F32),
        scratch_shapes=[pltpu.VMEM((SUBLANES + tm, width), F32),
                        pltpu.VMEM((tm, width), F32),
                        pltpu.VMEM((tm, width), F32),
                        pltpu.VMEM((1, width), F32)],
        compiler_params=_params("parallel", "arbitrary"),
        name="rglru_layer",
    )(x, *ops)


def _ffn_kernel(x_ref, gpre_ref, gpost_ref, wup_ref, cw_ref, cb_ref, wdn_ref, out_ref,
                halo_ref, us_ref, acc_ref, *, f_chunk):
    tm = acc_ref.shape[0]
    d_ff = wdn_ref.shape[0]
    taps = cw_ref.shape[0]

    @pl.when(pl.program_id(1) == 0)
    def _():
        halo_ref[...] = jnp.zeros_like(halo_ref)

    x = x_ref[0]
    hn = _rms(x, gpre_ref[...]).astype(BF16)
    for c in range(d_ff // f_chunk):
        buf = c % 2
        halves = []
        for part, off in enumerate((c * f_chunk, d_ff + c * f_chunk)):
            src = slice(off, off + f_chunk)
            dst = slice(part * f_chunk, (part + 1) * f_chunk)
            us_ref[buf, 0:SUBLANES, dst] = halo_ref[:, src]
            us_ref[buf, SUBLANES:, dst] = _dot(hn, wup_ref[:, src])
            u = cb_ref[:, src]
            for k in range(taps):
                start = SUBLANES - (taps - 1) + k
                u = u + us_ref[buf, start:start + tm, dst] * cw_ref[k:k + 1, src]
            halo_ref[:, src] = us_ref[buf, tm:tm + SUBLANES, dst]
            halves.append(u)
        act = (_gelu_tanh(halves[0]) * halves[1]).astype(BF16)
        contrib = _dot(act, wdn_ref[c * f_chunk:(c + 1) * f_chunk, :])
        if c == 0:
            acc_ref[...] = contrib
        else:
            acc_ref[...] += contrib
    out_ref[0] = x + _rms(acc_ref[...], gpost_ref[...])


def _ffn_layer(x, g_pre, g_post, w_up, conv_w, conv_b, w_down):
    batch, seq, d = x.shape
    d_ff = w_down.shape[0]
    f_chunk = 512
    tm = min(ROW_TILE, seq)
    tile = pl.BlockSpec((1, tm, d), lambda b, i: (b, i, 0))
    ops = [g_pre.reshape(1, d), g_post.reshape(1, d), w_up.astype(BF16), conv_w,
           conv_b.reshape(1, 2 * d_ff), w_down.astype(BF16)]
    return pl.pallas_call(
        functools.partial(_ffn_kernel, f_chunk=f_chunk),
        grid=(batch, seq // tm),
        in_specs=[tile] + [_resident(o.shape) for o in ops],
        out_specs=tile,
        out_shape=jax.ShapeDtypeStruct(x.shape, F32),
        scratch_shapes=[pltpu.VMEM((SUBLANES, 2 * d_ff), F32),
                        pltpu.VMEM((2, SUBLANES + tm, 2 * f_chunk), F32),
                        pltpu.VMEM((tm, d), F32)],
        compiler_params=_params("parallel", "arbitrary"),
        name="conv_ffn",
    )(x, *ops)


def kernel(x, norm_mix_pre, norm_mix_post, norm_ffn_pre, norm_ffn_post, rel_bias, attn_w_qkv, attn_w_o, lru_w_in, lru_conv_w, lru_conv_b, lru_ga_w, lru_ga_b, lru_gx_w, lru_gx_b, lru_lambda, lru_w_out, ffn_w_up, ffn_conv_w, ffn_conv_b, ffn_w_down):
    bias = _bias_tables(rel_bias)
    for layer in range(norm_mix_pre.shape[0]):
        j = layer // 2
        if layer % 2 == 0:
            x = _attention_layer(x, norm_mix_pre[layer], norm_mix_post[layer],
                                 attn_w_qkv[j], attn_w_o[j], bias)
        else:
            x = _lru_layer(x, norm_mix_pre[layer], norm_mix_post[layer], lru_w_in[j],
                           lru_conv_w[j], lru_conv_b[j], lru_ga_w[j], lru_ga_b[j],
                           lru_gx_w[j], lru_gx_b[j], lru_lambda[j], lru_w_out[j])
        x = _ffn_layer(x, norm_ffn_pre[layer], norm_ffn_post[layer], ffn_w_up[layer],
                       ffn_conv_w[layer], ffn_conv_b[layer], ffn_w_down[layer])
    return x
```

```python
import functools

import numpy as np
import jax
import jax.numpy as jnp
from jax import lax
from jax.experimental import pallas as pl
from jax.experimental.pallas import tpu as pltpu

F32 = jnp.float32
BF16 = jnp.bfloat16

RMS_EPS = 1e-6
NEG_INF = -1e30
HEAD_DIM = 128
HEADS_PER_GROUP = 8
WINDOWS = (128, 512, 2048)
DILATIONS = (1, 4, 16)
N_GROUPS = 3
BAND = 128
NUM_BUCKETS = 32
MAX_DISTANCE = 2048
LRU_C = 8.0
LRU_BLOCKS = 4
SUBLANES = 8
LANES = 128
VMEM_LIMIT_BYTES = 56 * 1024 * 1024
ROW_TILE = 512


def _params(*semantics):
    return pltpu.CompilerParams(dimension_semantics=semantics,
                                vmem_limit_bytes=VMEM_LIMIT_BYTES)


def _resident(shape):
    nd = len(shape)
    return pl.BlockSpec(shape, lambda *_: (0,) * nd, pipeline_mode=pl.Buffered(1))


def _rms(x, g):
    return x * lax.rsqrt(jnp.mean(x * x, axis=-1, keepdims=True) + RMS_EPS) * g


def _dot(a, b):
    return jnp.dot(a, b, preferred_element_type=F32)


def _gelu_tanh(x):
    c = float(np.sqrt(2.0 / np.pi))
    t = jnp.tanh(x * (c + (c * 0.044715) * (x * x)))
    return x * (0.5 + 0.5 * t)


def _sigmoid(x):
    return 1.0 / (1.0 + jnp.exp(-x))


def _t5_bucket(dist):
    max_exact = NUM_BUCKETS // 2
    d = np.maximum(dist, 1).astype(np.float64)
    large = max_exact + (np.log(d / max_exact) / np.log(MAX_DISTANCE / max_exact)
                         * (NUM_BUCKETS - max_exact)).astype(np.int32)
    large = np.minimum(large, NUM_BUCKETS - 1)
    return np.where(dist < max_exact, dist, large).astype(np.int32)


def _bucket_tables():
    i = np.arange(BAND)[:, None]
    k = np.arange(2 * BAND)[None, :]
    m = i + BAND - k
    valid = (m >= 0) & (m <= BAND)
    tabs = [np.where(valid, _t5_bucket(np.clip(m, 0, BAND) * d), -1) for d in DILATIONS]
    return np.stack(tabs).astype(np.int32)


def _bias_kernel(tbl_ref, bucket_ref, out_ref):
    h = pl.program_id(0)
    bucket = bucket_ref[0]
    acc = jnp.where(bucket < 0, NEG_INF, 0.0).astype(F32)
    for j in range(NUM_BUCKETS):
        acc = jnp.where(bucket == j, tbl_ref[j, h], acc)
    out_ref[0] = acc


def _bias_tables(rel_bias):
    n_heads = rel_bias.shape[1]
    buckets = jnp.asarray(_bucket_tables())
    return pl.pallas_call(
        _bias_kernel,
        grid=(n_heads,),
        in_specs=[pl.BlockSpec(memory_space=pltpu.SMEM),
                  pl.BlockSpec((1, BAND, 2 * BAND), lambda h: (h // HEADS_PER_GROUP, 0, 0))],
        out_specs=pl.BlockSpec((1, BAND, 2 * BAND), lambda h: (h, 0, 0)),
        out_shape=jax.ShapeDtypeStruct((n_heads, BAND, 2 * BAND), F32),
        compiler_params=_params("arbitrary"),
        name="bias_tables",
    )(rel_bias.astype(F32), buckets)


GROUP_WIDTH = HEADS_PER_GROUP * HEAD_DIM
N_SLABS = GROUP_WIDTH // LANES


def _qkv_kernel(x_ref, g_ref, w_ref, o0_ref, o1_ref, o2_ref, hs_ref, hp_ref):
    tm = x_ref.shape[1]
    h = _rms(x_ref[0], g_ref[...])
    hb = h.astype(BF16)
    for c in range(3):
        cols = slice(c * GROUP_WIDTH, (c + 1) * GROUP_WIDTH)
        o0_ref[0, 0, :, cols] = _dot(hb, w_ref[:, cols]).astype(o0_ref.dtype)
    for j in range(N_SLABS):
        hs_ref[j] = h[:, j * LANES:(j + 1) * LANES]
    for group, o_ref in ((1, o1_ref), (2, o2_ref)):
        d = DILATIONS[group]
        n = tm // d
        for r in range(d):
            for j in range(N_SLABS):
                hp_ref[r * n:(r + 1) * n, j * LANES:(j + 1) * LANES] = (
                    hs_ref[j, pl.ds(r, n, stride=d), :].astype(BF16))
        hp = hp_ref[...]
        for c in range(3):
            cols = slice(c * GROUP_WIDTH, (c + 1) * GROUP_WIDTH)
            wcols = slice((3 * group + c) * GROUP_WIDTH, (3 * group + c + 1) * GROUP_WIDTH)
            res = _dot(hp, w_ref[:, wcols]).astype(o_ref.dtype)
            for r in range(d):
                o_ref[0, r, :, cols] = res[r * n:(r + 1) * n, :]


def _qkv_proj(x, g, w):
    batch, seq, d_model = x.shape
    tm = min(ROW_TILE, seq)
    out_specs, out_shape = [], []
    for d in DILATIONS:
        out_specs.append(pl.BlockSpec((1, d, tm // d, 3 * GROUP_WIDTH), lambda b, i: (b, 0, i, 0)))
        out_shape.append(jax.ShapeDtypeStruct((batch, d, seq // d, 3 * GROUP_WIDTH), BF16))
    return pl.pallas_call(
        _qkv_kernel,
        grid=(batch, seq // tm),
        in_specs=[pl.BlockSpec((1, tm, d_model), lambda b, i: (b, i, 0)),
                  _resident((1, d_model)), _resident(w.shape)],
        out_specs=out_specs,
        out_shape=out_shape,
        scratch_shapes=[pltpu.VMEM((N_SLABS, tm, LANES), F32),
                        pltpu.VMEM((tm, d_model), BF16)],
        compiler_params=_params("parallel", "parallel"),
        name="qkv_proj",
    )(x, g.reshape(1, d_model), w)


def _attn_kernel(q_ref, k_ref, v_ref, kp_ref, vp_ref, bias_ref, o_ref, lse_ref, kk_ref, vv_ref):
    tq = q_ref.shape[2]
    first_tile = pl.program_id(2) == 0
    kk_ref[0:BAND, :] = kp_ref[0, 0]
    kk_ref[BAND:, :] = k_ref[0, 0]
    vv_ref[0:BAND, :] = vp_ref[0, 0]
    vv_ref[BAND:, :] = v_ref[0, 0]
    scale = HEAD_DIM ** -0.5
    lane = lax.broadcasted_iota(jnp.int32, (BAND, LANES), 1)
    key_col = lax.broadcasted_iota(jnp.int32, (BAND, 2 * BAND), 1)
    for j in range(tq // BAND):
        rows = slice(j * BAND, (j + 1) * BAND)
        win = slice(j * BAND, (j + 2) * BAND)
        lse_tile = jnp.zeros((BAND, LANES), F32)
        for h in range(HEADS_PER_GROUP):
            cols = slice(h * HEAD_DIM, (h + 1) * HEAD_DIM)
            logits = lax.dot_general(q_ref[0, 0, rows, cols], kk_ref[win, cols],
                                     (((1,), (1,)), ((), ())), preferred_element_type=F32)
            logits = logits * scale + bias_ref[h]
            if j == 0:
                logits = jnp.where(first_tile & (key_col < BAND), NEG_INF, logits)
            mx = jnp.max(logits, axis=-1, keepdims=True)
            p = jnp.exp(logits - mx)
            s = jnp.sum(p, axis=-1, keepdims=True)
            o = _dot(p.astype(BF16), vv_ref[win, cols]) / s
            o_ref[0, 0, rows, cols] = o.astype(o_ref.dtype)
            lse_tile = jnp.where(lane == h, mx + jnp.log(s), lse_tile)
        lse_ref[0, 0, rows, :] = lse_tile


def _group_attention(qkv, bias, group):
    batch, d, u, _ = qkv.shape
    tq = min(ROW_TILE, u)
    nblk = tq // BAND

    def slab(which):
        return lambda b, r, i: (b, r, i, which)

    def prev_slab(which):
        return lambda b, r, i: (b, r, jnp.maximum(i * nblk - 1, 0), which)

    return pl.pallas_call(
        _attn_kernel,
        grid=(batch, d, u // tq),
        in_specs=[pl.BlockSpec((1, 1, tq, GROUP_WIDTH), slab(0)),
                  pl.BlockSpec((1, 1, tq, GROUP_WIDTH), slab(1)),
                  pl.BlockSpec((1, 1, tq, GROUP_WIDTH), slab(2)),
                  pl.BlockSpec((1, 1, BAND, GROUP_WIDTH), prev_slab(1)),
                  pl.BlockSpec((1, 1, BAND, GROUP_WIDTH), prev_slab(2)),
                  pl.BlockSpec((HEADS_PER_GROUP, BAND, 2 * BAND), lambda b, r, i: (group, 0, 0))],
        out_specs=[pl.BlockSpec((1, 1, tq, GROUP_WIDTH), lambda b, r, i: (b, r, i, 0)),
                   pl.BlockSpec((1, 1, tq, LANES), lambda b, r, i: (b, r, i, 0))],
        out_shape=[jax.ShapeDtypeStruct((batch, d, u, GROUP_WIDTH), BF16),
                   jax.ShapeDtypeStruct((batch, d, u, LANES), F32)],
        scratch_shapes=[pltpu.VMEM((BAND + tq, GROUP_WIDTH), BF16),
                        pltpu.VMEM((BAND + tq, GROUP_WIDTH), BF16)],
        compiler_params=_params("parallel", "parallel", "parallel"),
        name=f"attn_group{group}",
    )(qkv, qkv, qkv, qkv, qkv, bias)


def _combine_kernel(x_ref, o0_ref, o1_ref, o2_ref, l0_ref, l1_ref, l2_ref, w_ref, g_ref,
                    out_ref, lt_ref, ot_ref):
    tm = x_ref.shape[1]
    for gi, (o_ref, l_ref) in enumerate(((o1_ref, l1_ref), (o2_ref, l2_ref))):
        d = DILATIONS[gi + 1]
        n = tm // d
        for r in range(d):
            lt_ref[gi, pl.ds(r, n, stride=d), :] = l_ref[0, r]
            for j in range(N_SLABS):
                ot_ref[gi, j, pl.ds(r, n, stride=d), :] = (
                    o_ref[0, r, :, j * LANES:(j + 1) * LANES].astype(F32))
    lses = [l0_ref[0, 0], lt_ref[0], lt_ref[1]]
    mx = jnp.maximum(jnp.maximum(lses[0], lses[1]), lses[2])
    es = [jnp.exp(l - mx) for l in lses]
    inv = 1.0 / (es[0] + es[1] + es[2])
    alphas = [e * inv for e in es]
    parts = []
    for h in range(HEADS_PER_GROUP):
        cols = slice(h * HEAD_DIM, (h + 1) * HEAD_DIM)
        acc = alphas[0][:, h:h + 1] * o0_ref[0, 0, :, cols].astype(F32)
        for gi in range(N_GROUPS - 1):
            acc = acc + alphas[gi + 1][:, h:h + 1] * ot_ref[gi, h]
        parts.append(acc.astype(BF16))
    o = jnp.concatenate(parts, axis=1)
    out_ref[0] = x_ref[0] + _rms(_dot(o, w_ref[...]), g_ref[...])


def _combine_project(x, outs, lses, w_o, g_post):
    batch, seq, d_model = x.shape
    tm = min(ROW_TILE, seq)
    tile = pl.BlockSpec((1, tm, d_model), lambda b, i: (b, i, 0))
    stream = lambda d, width: pl.BlockSpec((1, d, tm // d, width), lambda b, i: (b, 0, i, 0))
    return pl.pallas_call(
        _combine_kernel,
        grid=(batch, seq // tm),
        in_specs=[tile] + [stream(d, GROUP_WIDTH) for d in DILATIONS]
                 + [stream(d, LANES) for d in DILATIONS]
                 + [_resident(w_o.shape), _resident((1, d_model))],
        out_specs=tile,
        out_shape=jax.ShapeDtypeStruct(x.shape, F32),
        scratch_shapes=[pltpu.VMEM((N_GROUPS - 1, tm, LANES), F32),
                        pltpu.VMEM((N_GROUPS - 1, N_SLABS, tm, LANES), F32)],
        compiler_params=_params("parallel", "parallel"),
        name="attn_combine_out",
    )(x, *outs, *lses, w_o, g_post.reshape(1, d_model))


def _attention_layer(x, g_pre, g_post, w_qkv, w_o, bias):
    qkvs = _qkv_proj(x, g_pre, w_qkv.astype(BF16))
    outs, lses = [], []
    for group in range(N_GROUPS):
        o, lse = _group_attention(qkvs[group], bias, group)
        outs.append(o)
        lses.append(lse)
    return _combine_project(x, outs, lses, w_o.astype(BF16), g_post)


SEGMENTS = SUBLANES
SEG_PITCH_PAD = 8


def _to_interleaved(h, stage_ref, dst_ref):
    tm, d = h.shape
    seg = tm // SEGMENTS
    pitch = seg + SEG_PITCH_PAD
    for s in range(SEGMENTS):
        for j in range(d // LANES):
            stage_ref[j, s * pitch:s * pitch + seg, :] = h[s * seg:(s + 1) * seg, j * LANES:(j + 1) * LANES]
    for i in range(0, seg, 2):
        for j in range(d // LANES):
            pair = [stage_ref[j, pl.ds(i + e, SEGMENTS, stride=pitch), :] for e in range(2)]
            dst_ref[i * SEGMENTS:(i + 2) * SEGMENTS, j * LANES:(j + 1) * LANES] = (
                jnp.concatenate(pair, axis=0).astype(dst_ref.dtype))


def _residual_from_interleaved(y, stage_ref, x_ref, out_ref):
    tm, d = y.shape
    seg = tm // SEGMENTS
    pitch = seg + SEG_PITCH_PAD
    for i in range(seg):
        for j in range(d // LANES):
            stage_ref[j, pl.ds(i, SEGMENTS, stride=pitch), :] = (
                y[i * SEGMENTS:(i + 1) * SEGMENTS, j * LANES:(j + 1) * LANES])
    for s in range(SEGMENTS):
        rows = slice(s * seg, (s + 1) * seg)
        for j in range(d // LANES):
            cols = slice(j * LANES, (j + 1) * LANES)
            out_ref[0, rows, cols] = x_ref[0, rows, cols] + stage_ref[j, s * pitch:s * pitch + seg, :]


def _stage_conv_input(u, dst_ref, dst_idx, halo_ref, halo_cols, taps):
    tm, width = u.shape
    seg = tm // SEGMENTS
    head = (taps - 1) * SEGMENTS
    row = lax.broadcasted_iota(jnp.int32, (SEGMENTS, width), 0)
    dst_ref[dst_idx + (slice(head, head + tm),)] = u
    for k in range(1, taps):
        grp = u[(seg - k) * SEGMENTS:(seg - k + 1) * SEGMENTS, :]
        before = jnp.where(row == 0, halo_ref[k - 1:k, halo_cols], pltpu.roll(grp, 1, 0))
        dst_ref[dst_idx + (slice(head - k * SEGMENTS, head - (k - 1) * SEGMENTS),)] = before
        halo_ref[k - 1:k, halo_cols] = grp[SEGMENTS - 1:SEGMENTS, :]


def _causal_conv(src_ref, src_idx, tm, w_ref, w_cols, bias):
    taps = w_ref.shape[0]
    out = bias
    for t in range(taps):
        start = t * SEGMENTS
        out = out + src_ref[src_idx + (slice(start, start + tm),)] * w_ref[t:t + 1, w_cols]
    return out


def _log_sigmoid(x):
    return jnp.minimum(x, 0.0) - jnp.log1p(jnp.exp(-jnp.abs(x)))


def _sublane_scan(q, e):
    row = lax.broadcasted_iota(jnp.int32, q.shape, 0)
    for s in (1, 2, 4):
        q_prev = jnp.where(row >= s, pltpu.roll(q, s, 0), 1.0)
        e_prev = jnp.where(row >= s, pltpu.roll(e, s, 0), 0.0)
        e = q * e_prev + e
        q = q * q_prev
    return e


def _lru_kernel(x_ref, gpre_ref, gpost_ref, win_ref, cw_ref, cb_ref, gaw_ref, gab_ref,
                gxw_ref, gxb_ref, lam_ref, wout_ref, out_ref,
                halo_ref, stage_ref, hp_ref, xs_ref, a_ref, b_ref, p_ref, h_ref):
    tm, width = a_ref.shape
    seg = tm // SEGMENTS
    taps = cw_ref.shape[0]

    @pl.when(pl.program_id(1) == 0)
    def _():
        halo_ref[...] = jnp.zeros_like(halo_ref)
        h_ref[...] = jnp.zeros_like(h_ref)

    _to_interleaved(_rms(x_ref[0], gpre_ref[...]), stage_ref, hp_ref)
    hn = hp_ref[...]
    _stage_conv_input(_dot(hn, win_ref[:, :width]), xs_ref, (), halo_ref, slice(None), taps)
    gate = _gelu_tanh(_dot(hn, win_ref[:, width:]))
    xc = _causal_conv(xs_ref, (), tm, cw_ref, slice(None), cb_ref[...])

    xcb = xc.astype(BF16)
    blk = width // LRU_BLOCKS
    r_parts, i_parts = [], []
    for n in range(LRU_BLOCKS):
        xb = xcb[:, n * blk:(n + 1) * blk]
        r_parts.append(_sigmoid(_dot(xb, gaw_ref[n]) + gab_ref[n:n + 1, :]))
        i_parts.append(_sigmoid(_dot(xb, gxw_ref[n]) + gxb_ref[n:n + 1, :]))
    r = jnp.concatenate(r_parts, axis=1)
    gi = jnp.concatenate(i_parts, axis=1)
    log_a = LRU_C * r * _log_sigmoid(lam_ref[...])
    a_ref[...] = jnp.exp(log_a)
    mult = jnp.sqrt(jnp.maximum(1.0 - jnp.exp(2.0 * log_a), 0.0))
    b_ref[...] = mult * gi * xc

    def step(i, carry):
        h, p = carry
        rows = pl.ds(pl.multiple_of(i * SEGMENTS, SEGMENTS), SEGMENTS)
        a = a_ref[rows, :]
        h = a * h + b_ref[rows, :]
        p = a * p
        b_ref[rows, :] = h
        p_ref[rows, :] = p
        return h, p

    init = (jnp.zeros((SEGMENTS, width), F32), jnp.ones((SEGMENTS, width), F32))
    h_last, p_last = lax.fori_loop(0, seg, step, init, unroll=8)

    row = lax.broadcasted_iota(jnp.int32, (SEGMENTS, width), 0)
    e = jnp.where(row == 0, h_ref[...], pltpu.roll(h_last, 1, 0))
    q = jnp.where(row == 0, 0.0, pltpu.roll(p_last, 1, 0))
    entry = _sublane_scan(q, e)
    h_ref[...] = (h_last + p_last * entry)[SEGMENTS - 1:SEGMENTS, :]

    hs = (b_ref[...].reshape(seg, SEGMENTS, width)
          + p_ref[...].reshape(seg, SEGMENTS, width) * entry[None]).reshape(tm, width)
    y = _dot((hs * gate).astype(BF16), wout_ref[...])
    _residual_from_interleaved(_rms(y, gpost_ref[...]), stage_ref, x_ref, out_ref)


def _interleave_scratch(tm, d):
    seg = tm // SEGMENTS
    return [pltpu.VMEM((d // LANES, SEGMENTS * (seg + SEG_PITCH_PAD), LANES), F32),
            pltpu.VMEM((tm, d), BF16)]


def _lru_layer(x, g_pre, g_post, w_in, conv_w, conv_b, ga_w, ga_b, gx_w, gx_b, lam, w_out):
    batch, seq, d = x.shape
    width = w_out.shape[0]
    taps = conv_w.shape[0]
    tm = min(ROW_TILE, seq)
    tile = pl.BlockSpec((1, tm, d), lambda b, i: (b, i, 0))
    ops = [g_pre.reshape(1, d), g_post.reshape(1, d), w_in.astype(BF16), conv_w,
           conv_b.reshape(1, width), ga_w.astype(BF16), ga_b, gx_w.astype(BF16), gx_b,
           lam.reshape(1, width), w_out.astype(BF16)]
    return pl.pallas_call(
        _lru_kernel,
        grid=(batch, seq // tm),
        in_specs=[tile] + [_resident(o.shape) for o in ops],
        out_specs=tile,
        out_shape=jax.ShapeDtypeStruct(x.shape, F32),
        scratch_shapes=[pltpu.VMEM((taps - 1, width), F32)] + _interleave_scratch(tm, d)
                       + [pltpu.VMEM(((taps - 1) * SEGMENTS + tm, width), F32),
                          pltpu.VMEM((tm, width), F32),
                          pltpu.VMEM((tm, width), F32),
                          pltpu.VMEM((tm, width), F32),
                          pltpu.VMEM((1, width), F32)],
        compiler_params=_params("parallel", "arbitrary"),
        name="rglru_layer",
    )(x, *ops)


def _ffn_kernel(x_ref, gpre_ref, gpost_ref, wup_ref, cw_ref, cb_ref, wdn_ref, out_ref,
                halo_ref, stage_ref, hp_ref, us_ref, acc_ref, *, f_chunk):
    tm = acc_ref.shape[0]
    d_ff = wdn_ref.shape[0]
    taps = cw_ref.shape[0]

    @pl.when(pl.program_id(1) == 0)
    def _():
        halo_ref[...] = jnp.zeros_like(halo_ref)

    _to_interleaved(_rms(x_ref[0], gpre_ref[...]), stage_ref, hp_ref)
    hn = hp_ref[...]
    for c in range(d_ff // f_chunk):
        halves = []
        for part, off in enumerate((c * f_chunk, d_ff + c * f_chunk)):
            src = slice(off, off + f_chunk)
            dst = (c % 2, part)
            _stage_conv_input(_dot(hn, wup_ref[:, src]), us_ref, dst, halo_ref, src, taps)
            halves.append(_causal_conv(us_ref, dst, tm, cw_ref, src, cb_ref[:, src]))
        act = (_gelu_tanh(halves[0]) * halves[1]).astype(BF16)
        contrib = _dot(act, wdn_ref[c * f_chunk:(c + 1) * f_chunk, :])
        if c == 0:
            acc_ref[...] = contrib
        else:
            acc_ref[...] += contrib
    _residual_from_interleaved(_rms(acc_ref[...], gpost_ref[...]), stage_ref, x_ref, out_ref)


def _ffn_layer(x, g_pre, g_post, w_up, conv_w, conv_b, w_down):
    batch, seq, d = x.shape
    d_ff = w_down.shape[0]
    taps = conv_w.shape[0]
    f_chunk = 512
    tm = min(ROW_TILE, seq)
    tile = pl.BlockSpec((1, tm, d), lambda b, i: (b, i, 0))
    ops = [g_pre.reshape(1, d), g_post.reshape(1, d), w_up.astype(BF16), conv_w,
           conv_b.reshape(1, 2 * d_ff), w_down.astype(BF16)]
    return pl.pallas_call(
        functools.partial(_ffn_kernel, f_chunk=f_chunk),
        grid=(batch, seq // tm),
        in_specs=[tile] + [_resident(o.shape) for o in ops],
        out_specs=tile,
        out_shape=jax.ShapeDtypeStruct(x.shape, F32),
        scratch_shapes=[pltpu.VMEM((taps - 1, 2 * d_ff), F32)] + _interleave_scratch(tm, d)
                       + [pltpu.VMEM((2, 2, (taps - 1) * SEGMENTS + tm, f_chunk), F32),
                          pltpu.VMEM((tm, d), F32)],
        compiler_params=_params("parallel", "arbitrary"),
        name="conv_ffn",
    )(x, *ops)


def kernel(x, norm_mix_pre, norm_mix_post, norm_ffn_pre, norm_ffn_post, rel_bias, attn_w_qkv, attn_w_o, lru_w_in, lru_conv_w, lru_conv_b, lru_ga_w, lru_ga_b, lru_gx_w, lru_gx_b, lru_lambda, lru_w_out, ffn_w_up, ffn_conv_w, ffn_conv_b, ffn_w_down):
    bias = _bias_tables(rel_bias)
    for layer in range(norm_mix_pre.shape[0]):
        j = layer // 2
        if layer % 2 == 0:
            x = _attention_layer(x, norm_mix_pre[layer], norm_mix_post[layer],
                                 attn_w_qkv[j], attn_w_o[j], bias)
        else:
            x = _lru_layer(x, norm_mix_pre[layer], norm_mix_post[layer], lru_w_in[j],
                           lru_conv_w[j], lru_conv_b[j], lru_ga_w[j], lru_ga_b[j],
                           lru_gx_w[j], lru_gx_b[j], lru_lambda[j], lru_w_out[j])
        x = _ffn_layer(x, norm_ffn_pre[layer], norm_ffn_post[layer], ffn_w_up[layer],
                       ffn_conv_w[layer], ffn_conv_b[layer], ffn_w_down[layer])
    return x
```

```python
import functools

import numpy as np
import jax
import jax.numpy as jnp
from jax import lax
from jax.experimental import pallas as pl
from jax.experimental.pallas import tpu as pltpu

F32 = jnp.float32
BF16 = jnp.bfloat16

RMS_EPS = 1e-6
NEG_INF = -1e30
HEAD_DIM = 128
HEADS_PER_GROUP = 8
WINDOWS = (128, 512, 2048)
DILATIONS = (1, 4, 16)
N_GROUPS = 3
BAND = 128
NUM_BUCKETS = 32
MAX_DISTANCE = 2048
LRU_C = 8.0
LOG2E = float(np.log2(np.e))
Q_SCALE = HEAD_DIM ** -0.5 * LOG2E
LRU_BLOCKS = 4
SUBLANES = 8
LANES = 128
VMEM_LIMIT_BYTES = 56 * 1024 * 1024
ROW_TILE = 512


def _params(*semantics):
    return pltpu.CompilerParams(dimension_semantics=semantics,
                                vmem_limit_bytes=VMEM_LIMIT_BYTES)


def _resident(shape):
    nd = len(shape)
    return pl.BlockSpec(shape, lambda *_: (0,) * nd, pipeline_mode=pl.Buffered(1))


def _rms(x, g):
    return x * lax.rsqrt(jnp.mean(x * x, axis=-1, keepdims=True) + RMS_EPS) * g


def _dot(a, b):
    return jnp.dot(a, b, preferred_element_type=F32)


def _gelu_tanh(x):
    c = float(np.sqrt(2.0 / np.pi))
    t = jnp.tanh(x * (c + (c * 0.044715) * (x * x)))
    return x * (0.5 + 0.5 * t)


def _sigmoid(x):
    return 1.0 / (1.0 + jnp.exp(-x))


def _t5_bucket(dist):
    max_exact = NUM_BUCKETS // 2
    d = np.maximum(dist, 1).astype(np.float64)
    large = max_exact + (np.log(d / max_exact) / np.log(MAX_DISTANCE / max_exact)
                         * (NUM_BUCKETS - max_exact)).astype(np.int32)
    large = np.minimum(large, NUM_BUCKETS - 1)
    return np.where(dist < max_exact, dist, large).astype(np.int32)


def _bucket_tables():
    i = np.arange(BAND)[:, None]
    k = np.arange(2 * BAND)[None, :]
    m = i + BAND - k
    valid = (m >= 0) & (m <= BAND)
    tabs = [np.where(valid, _t5_bucket(np.clip(m, 0, BAND) * d), -1) for d in DILATIONS]
    return np.stack(tabs).astype(np.int32)


def _bias_kernel(tbl_ref, bucket_ref, out_ref):
    h = pl.program_id(0)
    bucket = bucket_ref[0]
    acc = jnp.where(bucket < 0, NEG_INF, 0.0).astype(F32)
    for j in range(NUM_BUCKETS):
        acc = jnp.where(bucket == j, tbl_ref[j, h], acc)
    out_ref[0] = acc * LOG2E


def _bias_tables(rel_bias):
    n_heads = rel_bias.shape[1]
    buckets = jnp.asarray(_bucket_tables())
    return pl.pallas_call(
        _bias_kernel,
        grid=(n_heads,),
        in_specs=[pl.BlockSpec(memory_space=pltpu.SMEM),
                  pl.BlockSpec((1, BAND, 2 * BAND), lambda h: (h // HEADS_PER_GROUP, 0, 0))],
        out_specs=pl.BlockSpec((1, BAND, 2 * BAND), lambda h: (h, 0, 0)),
        out_shape=jax.ShapeDtypeStruct((n_heads, BAND, 2 * BAND), F32),
        compiler_params=_params("arbitrary"),
        name="bias_tables",
    )(rel_bias.astype(F32), buckets)


GROUP_WIDTH = HEADS_PER_GROUP * HEAD_DIM
N_SLABS = GROUP_WIDTH // LANES


def _qkv_kernel(x_ref, g_ref, w_ref, o0_ref, o1_ref, o2_ref, hs_ref, hp_ref):
    tm = x_ref.shape[1]
    h = _rms(x_ref[0], g_ref[...])
    hb = h.astype(BF16)
    for c in range(3):
        cols = slice(c * GROUP_WIDTH, (c + 1) * GROUP_WIDTH)
        res = _dot(hb, w_ref[:, cols])
        o0_ref[0, 0, :, cols] = (res * Q_SCALE if c == 0 else res).astype(o0_ref.dtype)
    for j in range(N_SLABS):
        hs_ref[j] = h[:, j * LANES:(j + 1) * LANES]
    for group, o_ref in ((1, o1_ref), (2, o2_ref)):
        d = DILATIONS[group]
        n = tm // d
        for r in range(d):
            for j in range(N_SLABS):
                hp_ref[r * n:(r + 1) * n, j * LANES:(j + 1) * LANES] = (
                    hs_ref[j, pl.ds(r, n, stride=d), :].astype(BF16))
        hp = hp_ref[...]
        for c in range(3):
            cols = slice(c * GROUP_WIDTH, (c + 1) * GROUP_WIDTH)
            wcols = slice((3 * group + c) * GROUP_WIDTH, (3 * group + c + 1) * GROUP_WIDTH)
            res = _dot(hp, w_ref[:, wcols])
            res = (res * Q_SCALE if c == 0 else res).astype(o_ref.dtype)
            for r in range(d):
                o_ref[0, r, :, cols] = res[r * n:(r + 1) * n, :]


def _qkv_proj(x, g, w):
    batch, seq, d_model = x.shape
    tm = min(ROW_TILE, seq)
    out_specs, out_shape = [], []
    for d in DILATIONS:
        out_specs.append(pl.BlockSpec((1, d, tm // d, 3 * GROUP_WIDTH), lambda b, i: (b, 0, i, 0)))
        out_shape.append(jax.ShapeDtypeStruct((batch, d, seq // d, 3 * GROUP_WIDTH), BF16))
    return pl.pallas_call(
        _qkv_kernel,
        grid=(batch, seq // tm),
        in_specs=[pl.BlockSpec((1, tm, d_model), lambda b, i: (b, i, 0)),
                  _resident((1, d_model)), _resident(w.shape)],
        out_specs=out_specs,
        out_shape=out_shape,
        scratch_shapes=[pltpu.VMEM((N_SLABS, tm, LANES), F32),
                        pltpu.VMEM((tm, d_model), BF16)],
        compiler_params=_params("parallel", "parallel"),
        name="qkv_proj",
    )(x, g.reshape(1, d_model), w)


def _attn_kernel(q_ref, k_ref, v_ref, kp_ref, vp_ref, bias_ref, o_ref, lse_ref, kk_ref, vv_ref):
    tq = q_ref.shape[2]
    first_tile = pl.program_id(2) == 0
    kk_ref[0:BAND, :] = kp_ref[0, 0]
    kk_ref[BAND:, :] = k_ref[0, 0]
    ones = jnp.ones((BAND + tq, HEAD_DIM), BF16)
    for h in range(HEADS_PER_GROUP):
        cols = slice(h * HEAD_DIM, (h + 1) * HEAD_DIM)
        vv_ref[0:BAND, 2 * h * HEAD_DIM:(2 * h + 1) * HEAD_DIM] = vp_ref[0, 0, :, cols]
        vv_ref[BAND:, 2 * h * HEAD_DIM:(2 * h + 1) * HEAD_DIM] = v_ref[0, 0, :, cols]
        vv_ref[:, (2 * h + 1) * HEAD_DIM:(2 * h + 2) * HEAD_DIM] = ones
    lane = lax.broadcasted_iota(jnp.int32, (BAND, LANES), 1)
    key_col = lax.broadcasted_iota(jnp.int32, (BAND, 2 * BAND), 1)
    for j in range(tq // BAND):
        rows = slice(j * BAND, (j + 1) * BAND)
        win = slice(j * BAND, (j + 2) * BAND)
        lse_tile = jnp.zeros((BAND, LANES), F32)
        for h in range(HEADS_PER_GROUP):
            cols = slice(h * HEAD_DIM, (h + 1) * HEAD_DIM)
            logits = lax.dot_general(q_ref[0, 0, rows, cols], kk_ref[win, cols],
                                     (((1,), (1,)), ((), ())), preferred_element_type=F32)
            logits = logits + bias_ref[h]
            if j == 0:
                logits = jnp.where(first_tile & (key_col < BAND), NEG_INF, logits)
            mx = jnp.max(logits, axis=-1, keepdims=True)
            p = jnp.exp2(logits - mx)
            pv = _dot(p.astype(BF16), vv_ref[win, 2 * h * HEAD_DIM:(2 * h + 2) * HEAD_DIM])
            s = pv[:, HEAD_DIM:]
            o_ref[0, 0, rows, cols] = (pv[:, :HEAD_DIM] / s).astype(o_ref.dtype)
            lse_tile = jnp.where(lane == h, mx + jnp.log2(s), lse_tile)
        lse_ref[0, 0, rows, :] = lse_tile


def _group_attention(qkv, bias, group):
    batch, d, u, _ = qkv.shape
    tq = min(ROW_TILE, u)
    nblk = tq // BAND

    def slab(which):
        return lambda b, r, i: (b, r, i, which)

    def prev_slab(which):
        return lambda b, r, i: (b, r, jnp.maximum(i * nblk - 1, 0), which)

    return pl.pallas_call(
        _attn_kernel,
        grid=(batch, d, u // tq),
        in_specs=[pl.BlockSpec((1, 1, tq, GROUP_WIDTH), slab(0)),
                  pl.BlockSpec((1, 1, tq, GROUP_WIDTH), slab(1)),
                  pl.BlockSpec((1, 1, tq, GROUP_WIDTH), slab(2)),
                  pl.BlockSpec((1, 1, BAND, GROUP_WIDTH), prev_slab(1)),
                  pl.BlockSpec((1, 1, BAND, GROUP_WIDTH), prev_slab(2)),
                  pl.BlockSpec((HEADS_PER_GROUP, BAND, 2 * BAND), lambda b, r, i: (group, 0, 0))],
        out_specs=[pl.BlockSpec((1, 1, tq, GROUP_WIDTH), lambda b, r, i: (b, r, i, 0)),
                   pl.BlockSpec((1, 1, tq, LANES), lambda b, r, i: (b, r, i, 0))],
        out_shape=[jax.ShapeDtypeStruct((batch, d, u, GROUP_WIDTH), BF16),
                   jax.ShapeDtypeStruct((batch, d, u, LANES), F32)],
        scratch_shapes=[pltpu.VMEM((BAND + tq, GROUP_WIDTH), BF16),
                        pltpu.VMEM((BAND + tq, 2 * GROUP_WIDTH), BF16)],
        compiler_params=_params("parallel", "parallel", "parallel"),
        name=f"attn_group{group}",
    )(qkv, qkv, qkv, qkv, qkv, bias)


def _combine_kernel(x_ref, o0_ref, o1_ref, o2_ref, l0_ref, l1_ref, l2_ref, w_ref, g_ref,
                    out_ref, lt_ref, ot_ref):
    tm = x_ref.shape[1]
    for gi, (o_ref, l_ref) in enumerate(((o1_ref, l1_ref), (o2_ref, l2_ref))):
        d = DILATIONS[gi + 1]
        n = tm // d
        for r in range(d):
            lt_ref[gi, pl.ds(r, n, stride=d), :] = l_ref[0, r]
            for j in range(N_SLABS):
                ot_ref[gi, j, pl.ds(r, n, stride=d), :] = (
                    o_ref[0, r, :, j * LANES:(j + 1) * LANES].astype(F32))
    lses = [l0_ref[0, 0], lt_ref[0], lt_ref[1]]
    mx = jnp.maximum(jnp.maximum(lses[0], lses[1]), lses[2])
    es = [jnp.exp2(l - mx) for l in lses]
    inv = 1.0 / (es[0] + es[1] + es[2])
    alphas = [e * inv for e in es[1:]]
    parts = []
    for h in range(HEADS_PER_GROUP):
        base = o0_ref[0, 0, :, h * HEAD_DIM:(h + 1) * HEAD_DIM].astype(F32)
        acc = base
        for gi in range(N_GROUPS - 1):
            acc = acc + alphas[gi][:, h:h + 1] * (ot_ref[gi, h] - base)
        parts.append(acc.astype(BF16))
    o = jnp.concatenate(parts, axis=1)
    out_ref[0] = x_ref[0] + _rms(_dot(o, w_ref[...]), g_ref[...])


def _combine_project(x, outs, lses, w_o, g_post):
    batch, seq, d_model = x.shape
    tm = min(ROW_TILE, seq)
    tile = pl.BlockSpec((1, tm, d_model), lambda b, i: (b, i, 0))
    stream = lambda d, width: pl.BlockSpec((1, d, tm // d, width), lambda b, i: (b, 0, i, 0))
    return pl.pallas_call(
        _combine_kernel,
        grid=(batch, seq // tm),
        in_specs=[tile] + [stream(d, GROUP_WIDTH) for d in DILATIONS]
                 + [stream(d, LANES) for d in DILATIONS]
                 + [_resident(w_o.shape), _resident((1, d_model))],
        out_specs=tile,
        out_shape=jax.ShapeDtypeStruct(x.shape, F32),
        scratch_shapes=[pltpu.VMEM((N_GROUPS - 1, tm, LANES), F32),
                        pltpu.VMEM((N_GROUPS - 1, N_SLABS, tm, LANES), F32)],
        compiler_params=_params("parallel", "parallel"),
        name="attn_combine_out",
    )(x, *outs, *lses, w_o, g_post.reshape(1, d_model))


def _attention_layer(x, g_pre, g_post, w_qkv, w_o, bias):
    qkvs = _qkv_proj(x, g_pre, w_qkv.astype(BF16))
    outs, lses = [], []
    for group in range(N_GROUPS):
        o, lse = _group_attention(qkvs[group], bias, group)
        outs.append(o)
        lses.append(lse)
    return _combine_project(x, outs, lses, w_o.astype(BF16), g_post)


SEGMENTS = SUBLANES
SEG_PITCH_PAD = 8


def _to_interleaved(h, stage_ref, dst_ref):
    tm, d = h.shape
    seg = tm // SEGMENTS
    pitch = seg + SEG_PITCH_PAD
    for s in range(SEGMENTS):
        for j in range(d // LANES):
            stage_ref[j, s * pitch:s * pitch + seg, :] = h[s * seg:(s + 1) * seg, j * LANES:(j + 1) * LANES]
    for i in range(0, seg, 2):
        for j in range(d // LANES):
            pair = [stage_ref[j, pl.ds(i + e, SEGMENTS, stride=pitch), :] for e in range(2)]
            dst_ref[i * SEGMENTS:(i + 2) * SEGMENTS, j * LANES:(j + 1) * LANES] = (
                jnp.concatenate(pair, axis=0).astype(dst_ref.dtype))


def _residual_from_interleaved(y, stage_ref, x_ref, out_ref):
    tm, d = y.shape
    seg = tm // SEGMENTS
    pitch = seg + SEG_PITCH_PAD
    for i in range(seg):
        for j in range(d // LANES):
            stage_ref[j, pl.ds(i, SEGMENTS, stride=pitch), :] = (
                y[i * SEGMENTS:(i + 1) * SEGMENTS, j * LANES:(j + 1) * LANES])
    for s in range(SEGMENTS):
        rows = slice(s * seg, (s + 1) * seg)
        for j in range(d // LANES):
            cols = slice(j * LANES, (j + 1) * LANES)
            out_ref[0, rows, cols] = x_ref[0, rows, cols] + stage_ref[j, s * pitch:s * pitch + seg, :]


def _stage_conv_input(u, dst_ref, dst_cols, halo_ref, halo_cols, taps):
    tm, width = u.shape
    seg = tm // SEGMENTS
    head = (taps - 1) * SEGMENTS
    row = lax.broadcasted_iota(jnp.int32, (SEGMENTS, width), 0)
    dst_ref[head:head + tm, dst_cols] = u
    for k in range(1, taps):
        grp = u[(seg - k) * SEGMENTS:(seg - k + 1) * SEGMENTS, :]
        before = jnp.where(row == 0, halo_ref[k - 1:k, halo_cols], pltpu.roll(grp, 1, 0))
        dst_ref[head - k * SEGMENTS:head - (k - 1) * SEGMENTS, dst_cols] = before
        halo_ref[k - 1:k, halo_cols] = grp[SEGMENTS - 1:SEGMENTS, :]


def _causal_conv(src_ref, src_cols, tm, w_ref, w_cols, bias):
    taps = w_ref.shape[0]
    out = bias
    for t in range(taps):
        start = t * SEGMENTS
        out = out + src_ref[start:start + tm, src_cols] * w_ref[t:t + 1, w_cols]
    return out


def _log_sigmoid(x):
    return jnp.minimum(x, 0.0) - jnp.log1p(jnp.exp(-jnp.abs(x)))


def _sublane_scan(q, e):
    row = lax.broadcasted_iota(jnp.int32, q.shape, 0)
    for s in (1, 2, 4):
        q_prev = jnp.where(row >= s, pltpu.roll(q, s, 0), 1.0)
        e_prev = jnp.where(row >= s, pltpu.roll(e, s, 0), 0.0)
        e = q * e_prev + e
        q = q * q_prev
    return e


def _lru_kernel(x_ref, gpre_ref, gpost_ref, win_ref, cw_ref, cb_ref, gaw_ref, gab_ref,
                gxw_ref, gxb_ref, lam_ref, wout_ref, out_ref,
                halo_ref, stage_ref, hp_ref, xs_ref, a_ref, b_ref, p_ref, h_ref):
    tm, width = a_ref.shape
    seg = tm // SEGMENTS
    taps = cw_ref.shape[0]

    @pl.when(pl.program_id(1) == 0)
    def _():
        halo_ref[...] = jnp.zeros_like(halo_ref)
        h_ref[...] = jnp.zeros_like(h_ref)

    _to_interleaved(_rms(x_ref[0], gpre_ref[...]), stage_ref, hp_ref)
    hn = hp_ref[...]
    _stage_conv_input(_dot(hn, win_ref[:, :width]), xs_ref, slice(None), halo_ref, slice(None), taps)
    gate = _gelu_tanh(_dot(hn, win_ref[:, width:]))
    xc = _causal_conv(xs_ref, slice(None), tm, cw_ref, slice(None), cb_ref[...])

    xcb = xc.astype(BF16)
    blk = width // LRU_BLOCKS
    r_parts, i_parts = [], []
    for n in range(LRU_BLOCKS):
        xb = xcb[:, n * blk:(n + 1) * blk]
        r_parts.append(_sigmoid(_dot(xb, gaw_ref[n]) + gab_ref[n:n + 1, :]))
        i_parts.append(_sigmoid(_dot(xb, gxw_ref[n]) + gxb_ref[n:n + 1, :]))
    r = jnp.concatenate(r_parts, axis=1)
    gi = jnp.concatenate(i_parts, axis=1)
    a = jnp.exp2(r * ((LRU_C * LOG2E) * _log_sigmoid(lam_ref[...])))
    a_ref[...] = a
    mult = jnp.exp2(0.5 * jnp.log2(jnp.maximum(1.0 - a * a, 0.0)))
    b_ref[...] = mult * gi * xc

    def step(i, carry):
        h, p = carry
        rows = pl.ds(pl.multiple_of(i * SEGMENTS, SEGMENTS), SEGMENTS)
        a = a_ref[rows, :]
        h = a * h + b_ref[rows, :]
        p = a * p
        b_ref[rows, :] = h
        p_ref[rows, :] = p
        return h, p

    init = (jnp.zeros((SEGMENTS, width), F32), jnp.ones((SEGMENTS, width), F32))
    h_last, p_last = lax.fori_loop(0, seg, step, init, unroll=8)

    row = lax.broadcasted_iota(jnp.int32, (SEGMENTS, width), 0)
    e = jnp.where(row == 0, h_ref[...], pltpu.roll(h_last, 1, 0))
    q = jnp.where(row == 0, 0.0, pltpu.roll(p_last, 1, 0))
    entry = _sublane_scan(q, e)
    h_ref[...] = (h_last + p_last * entry)[SEGMENTS - 1:SEGMENTS, :]

    hs = (b_ref[...].reshape(seg, SEGMENTS, width)
          + p_ref[...].reshape(seg, SEGMENTS, width) * entry[None]).reshape(tm, width)
    y = _dot((hs * gate).astype(BF16), wout_ref[...])
    _residual_from_interleaved(_rms(y, gpost_ref[...]), stage_ref, x_ref, out_ref)


def _interleave_scratch(tm, d):
    seg = tm // SEGMENTS
    return [pltpu.VMEM((d // LANES, SEGMENTS * (seg + SEG_PITCH_PAD), LANES), F32),
            pltpu.VMEM((tm, d), BF16)]


def _lru_layer(x, g_pre, g_post, w_in, conv_w, conv_b, ga_w, ga_b, gx_w, gx_b, lam, w_out):
    batch, seq, d = x.shape
    width = w_out.shape[0]
    taps = conv_w.shape[0]
    tm = min(ROW_TILE, seq)
    tile = pl.BlockSpec((1, tm, d), lambda b, i: (b, i, 0))
    ops = [g_pre.reshape(1, d), g_post.reshape(1, d), w_in.astype(BF16), conv_w,
           conv_b.reshape(1, width), ga_w.astype(BF16), ga_b, gx_w.astype(BF16), gx_b,
           lam.reshape(1, width), w_out.astype(BF16)]
    return pl.pallas_call(
        _lru_kernel,
        grid=(batch, seq // tm),
        in_specs=[tile] + [_resident(o.shape) for o in ops],
        out_specs=tile,
        out_shape=jax.ShapeDtypeStruct(x.shape, F32),
        scratch_shapes=[pltpu.VMEM((taps - 1, width), F32)] + _interleave_scratch(tm, d)
                       + [pltpu.VMEM(((taps - 1) * SEGMENTS + tm, width), F32),
                          pltpu.VMEM((tm, width), F32),
                          pltpu.VMEM((tm, width), F32),
                          pltpu.VMEM((tm, width), F32),
                          pltpu.VMEM((1, width), F32)],
        compiler_params=_params("parallel", "arbitrary"),
        name="rglru_layer",
    )(x, *ops)


def _ffn_kernel(x_ref, gpre_ref, gpost_ref, wup_ref, cw_ref, cb_ref, wdn_ref, out_ref,
                halo_ref, stage_ref, hp_ref, us_a0, us_a1, us_b0, us_b1, acc_ref, *, f_chunk):
    us_refs = ((us_a0, us_a1), (us_b0, us_b1))
    tm = acc_ref.shape[0]
    d_ff = wdn_ref.shape[0]
    taps = cw_ref.shape[0]

    @pl.when(pl.program_id(1) == 0)
    def _():
        halo_ref[...] = jnp.zeros_like(halo_ref)

    _to_interleaved(_rms(x_ref[0], gpre_ref[...]), stage_ref, hp_ref)
    hn = hp_ref[...]
    for c in range(d_ff // f_chunk):
        halves = []
        for part, off in enumerate((c * f_chunk, d_ff + c * f_chunk)):
            src = slice(off, off + f_chunk)
            us_ref = us_refs[c % 2][part]
            _stage_conv_input(_dot(hn, wup_ref[:, src]), us_ref, slice(None), halo_ref, src, taps)
            halves.append(_causal_conv(us_ref, slice(None), tm, cw_ref, src, cb_ref[:, src]))
        act = (_gelu_tanh(halves[0]) * halves[1]).astype(BF16)
        contrib = _dot(act, wdn_ref[c * f_chunk:(c + 1) * f_chunk, :])
        if c == 0:
            acc_ref[...] = contrib
        else:
            acc_ref[...] += contrib
    _residual_from_interleaved(_rms(acc_ref[...], gpost_ref[...]), stage_ref, x_ref, out_ref)


def _ffn_layer(x, g_pre, g_post, w_up, conv_w, conv_b, w_down):
    batch, seq, d = x.shape
    d_ff = w_down.shape[0]
    taps = conv_w.shape[0]
    f_chunk = 1024
    tm = min(ROW_TILE, seq)
    tile = pl.BlockSpec((1, tm, d), lambda b, i: (b, i, 0))
    ops = [g_pre.reshape(1, d), g_post.reshape(1, d), w_up.astype(BF16), conv_w,
           conv_b.reshape(1, 2 * d_ff), w_down.astype(BF16)]
    return pl.pallas_call(
        functools.partial(_ffn_kernel, f_chunk=f_chunk),
        grid=(batch, seq // tm),
        in_specs=[tile] + [_resident(o.shape) for o in ops],
        out_specs=tile,
        out_shape=jax.ShapeDtypeStruct(x.shape, F32),
        scratch_shapes=[pltpu.VMEM((taps - 1, 2 * d_ff), F32)] + _interleave_scratch(tm, d)
                       + [pltpu.VMEM(((taps - 1) * SEGMENTS + tm, f_chunk), F32)] * 4
                       + [pltpu.VMEM((tm, d), F32)],
        compiler_params=_params("parallel", "arbitrary"),
        name="conv_ffn",
    )(x, *ops)


def kernel(x, norm_mix_pre, norm_mix_post, norm_ffn_pre, norm_ffn_post, rel_bias, attn_w_qkv, attn_w_o, lru_w_in, lru_conv_w, lru_conv_b, lru_ga_w, lru_ga_b, lru_gx_w, lru_gx_b, lru_lambda, lru_w_out, ffn_w_up, ffn_conv_w, ffn_conv_b, ffn_w_down):
    bias = _bias_tables(rel_bias)
    for layer in range(norm_mix_pre.shape[0]):
        j = layer // 2
        if layer % 2 == 0:
            x = _attention_layer(x, norm_mix_pre[layer], norm_mix_post[layer],
                                 attn_w_qkv[j], attn_w_o[j], bias)
        else:
            x = _lru_layer(x, norm_mix_pre[layer], norm_mix_post[layer], lru_w_in[j],
                           lru_conv_w[j], lru_conv_b[j], lru_ga_w[j], lru_ga_b[j],
                           lru_gx_w[j], lru_gx_b[j], lru_lambda[j], lru_w_out[j])
        x = _ffn_layer(x, norm_ffn_pre[layer], norm_ffn_post[layer], ffn_w_up[layer],
                       ffn_conv_w[layer], ffn_conv_b[layer], ffn_w_down[layer])
    return x
```

```python
import functools

import numpy as np
import jax
import jax.numpy as jnp
from jax import lax
from jax.experimental import pallas as pl
from jax.experimental.pallas import tpu as pltpu

F32 = jnp.float32
BF16 = jnp.bfloat16

RMS_EPS = 1e-6
NEG_INF = -1e30
HEAD_DIM = 128
HEADS_PER_GROUP = 8
WINDOWS = (128, 512, 2048)
DILATIONS = (1, 4, 16)
N_GROUPS = 3
BAND = 128
NUM_BUCKETS = 32
MAX_DISTANCE = 2048
LRU_C = 8.0
LOG2E = float(np.log2(np.e))
Q_SCALE = HEAD_DIM ** -0.5 * LOG2E
LRU_BLOCKS = 4
SUBLANES = 8
LANES = 128
VMEM_LIMIT_BYTES = 56 * 1024 * 1024
ROW_TILE = 512
ATTN_TILE = 1024


def _params(*semantics):
    return pltpu.CompilerParams(dimension_semantics=semantics,
                                vmem_limit_bytes=VMEM_LIMIT_BYTES)


def _resident(shape):
    nd = len(shape)
    return pl.BlockSpec(shape, lambda *_: (0,) * nd, pipeline_mode=pl.Buffered(1))


def _rms(x, g):
    return x * lax.rsqrt(jnp.mean(x * x, axis=-1, keepdims=True) + RMS_EPS) * g


def _dot(a, b):
    return jnp.dot(a, b, preferred_element_type=F32)


def _gelu_tanh(x):
    c = float(np.sqrt(2.0 / np.pi))
    t = jnp.tanh(x * (c + (c * 0.044715) * (x * x)))
    return x * (0.5 + 0.5 * t)


def _sigmoid(x):
    return 1.0 / (1.0 + jnp.exp(-x))


def _t5_bucket(dist):
    max_exact = NUM_BUCKETS // 2
    d = np.maximum(dist, 1).astype(np.float64)
    large = max_exact + (np.log(d / max_exact) / np.log(MAX_DISTANCE / max_exact)
                         * (NUM_BUCKETS - max_exact)).astype(np.int32)
    large = np.minimum(large, NUM_BUCKETS - 1)
    return np.where(dist < max_exact, dist, large).astype(np.int32)


def _bucket_tables():
    i = np.arange(BAND)[:, None]
    k = np.arange(2 * BAND)[None, :]
    m = i + BAND - k
    valid = (m >= 0) & (m <= BAND)
    tabs = [np.where(valid, _t5_bucket(np.clip(m, 0, BAND) * d), -1) for d in DILATIONS]
    return np.stack(tabs).astype(np.int32)


def _bias_kernel(tbl_ref, bucket_ref, out_ref):
    h = pl.program_id(0)
    bucket = bucket_ref[0]
    acc = jnp.where(bucket < 0, NEG_INF, 0.0).astype(F32)
    for j in range(NUM_BUCKETS):
        acc = jnp.where(bucket == j, tbl_ref[j, h], acc)
    out_ref[0] = acc * LOG2E


def _bias_tables(rel_bias):
    n_heads = rel_bias.shape[1]
    buckets = jnp.asarray(_bucket_tables())
    return pl.pallas_call(
        _bias_kernel,
        grid=(n_heads,),
        in_specs=[pl.BlockSpec(memory_space=pltpu.SMEM),
                  pl.BlockSpec((1, BAND, 2 * BAND), lambda h: (h // HEADS_PER_GROUP, 0, 0))],
        out_specs=pl.BlockSpec((1, BAND, 2 * BAND), lambda h: (h, 0, 0)),
        out_shape=jax.ShapeDtypeStruct((n_heads, BAND, 2 * BAND), F32),
        compiler_params=_params("arbitrary"),
        name="bias_tables",
    )(rel_bias.astype(F32), buckets)


GROUP_WIDTH = HEADS_PER_GROUP * HEAD_DIM
N_SLABS = GROUP_WIDTH // LANES


def _qkv_kernel(x_ref, g_ref, w_ref, o0_ref, o1_ref, o2_ref, hs_ref, hp_ref):
    tm = x_ref.shape[1]
    h = _rms(x_ref[0], g_ref[...])
    hb = h.astype(BF16)
    for c in range(3):
        cols = slice(c * GROUP_WIDTH, (c + 1) * GROUP_WIDTH)
        res = _dot(hb, w_ref[:, cols])
        o0_ref[0, 0, :, cols] = (res * Q_SCALE if c == 0 else res).astype(o0_ref.dtype)
    for j in range(N_SLABS):
        hs_ref[j] = h[:, j * LANES:(j + 1) * LANES]
    for group, o_ref in ((1, o1_ref), (2, o2_ref)):
        d = DILATIONS[group]
        n = tm // d
        for r in range(d):
            for j in range(N_SLABS):
                hp_ref[r * n:(r + 1) * n, j * LANES:(j + 1) * LANES] = (
                    hs_ref[j, pl.ds(r, n, stride=d), :].astype(BF16))
        hp = hp_ref[...]
        for c in range(3):
            cols = slice(c * GROUP_WIDTH, (c + 1) * GROUP_WIDTH)
            wcols = slice((3 * group + c) * GROUP_WIDTH, (3 * group + c + 1) * GROUP_WIDTH)
            res = _dot(hp, w_ref[:, wcols])
            res = (res * Q_SCALE if c == 0 else res).astype(o_ref.dtype)
            for r in range(d):
                o_ref[0, r, :, cols] = res[r * n:(r + 1) * n, :]


def _qkv_proj(x, g, w):
    batch, seq, d_model = x.shape
    tm = min(ROW_TILE, seq)
    out_specs, out_shape = [], []
    for d in DILATIONS:
        out_specs.append(pl.BlockSpec((1, d, tm // d, 3 * GROUP_WIDTH), lambda b, i: (b, 0, i, 0)))
        out_shape.append(jax.ShapeDtypeStruct((batch, d, seq // d, 3 * GROUP_WIDTH), BF16))
    return pl.pallas_call(
        _qkv_kernel,
        grid=(batch, seq // tm),
        in_specs=[pl.BlockSpec((1, tm, d_model), lambda b, i: (b, i, 0)),
                  _resident((1, d_model)), _resident(w.shape)],
        out_specs=out_specs,
        out_shape=out_shape,
        scratch_shapes=[pltpu.VMEM((N_SLABS, tm, LANES), F32),
                        pltpu.VMEM((tm, d_model), BF16)],
        compiler_params=_params("parallel", "parallel"),
        name="qkv_proj",
    )(x, g.reshape(1, d_model), w)


def _attn_kernel(q_ref, k_ref, v_ref, kp_ref, vp_ref, bias_ref, o_ref, lse_ref, kk_ref, vv_ref):
    tq = q_ref.shape[2]
    first_tile = pl.program_id(2) == 0
    kk_ref[0:BAND, :] = kp_ref[0, 0]
    kk_ref[BAND:, :] = k_ref[0, 0]
    ones = jnp.ones((BAND + tq, HEAD_DIM), BF16)
    for h in range(HEADS_PER_GROUP):
        cols = slice(h * HEAD_DIM, (h + 1) * HEAD_DIM)
        vv_ref[0:BAND, 2 * h * HEAD_DIM:(2 * h + 1) * HEAD_DIM] = vp_ref[0, 0, :, cols]
        vv_ref[BAND:, 2 * h * HEAD_DIM:(2 * h + 1) * HEAD_DIM] = v_ref[0, 0, :, cols]
        vv_ref[:, (2 * h + 1) * HEAD_DIM:(2 * h + 2) * HEAD_DIM] = ones
    lane = lax.broadcasted_iota(jnp.int32, (BAND, LANES), 1)
    key_col = lax.broadcasted_iota(jnp.int32, (BAND, 2 * BAND), 1)
    for j in range(tq // BAND):
        rows = slice(j * BAND, (j + 1) * BAND)
        win = slice(j * BAND, (j + 2) * BAND)
        lse_tile = jnp.zeros((BAND, LANES), F32)
        for h in range(HEADS_PER_GROUP):
            cols = slice(h * HEAD_DIM, (h + 1) * HEAD_DIM)
            logits = lax.dot_general(q_ref[0, 0, rows, cols], kk_ref[win, cols],
                                     (((1,), (1,)), ((), ())), preferred_element_type=F32)
            logits = logits + bias_ref[h]
            if j == 0:
                logits = jnp.where(first_tile & (key_col < BAND), NEG_INF, logits)
            mx = jnp.max(logits, axis=-1, keepdims=True)
            p = jnp.exp2(logits - mx)
            pv = _dot(p.astype(BF16), vv_ref[win, 2 * h * HEAD_DIM:(2 * h + 2) * HEAD_DIM])
            s = pv[:, HEAD_DIM:]
            o_ref[0, 0, rows, cols] = (pv[:, :HEAD_DIM] / s).astype(o_ref.dtype)
            lse_tile = jnp.where(lane == h, mx + jnp.log2(s), lse_tile)
        lse_ref[0, 0, rows, :] = lse_tile


def _group_attention(qkv, bias, group):
    batch, d, u, _ = qkv.shape
    tq = min(ATTN_TILE, u)
    nblk = tq // BAND

    def slab(which):
        return lambda b, r, i: (b, r, i, which)

    def prev_slab(which):
        return lambda b, r, i: (b, r, jnp.maximum(i * nblk - 1, 0), which)

    return pl.pallas_call(
        _attn_kernel,
        grid=(batch, d, u // tq),
        in_specs=[pl.BlockSpec((1, 1, tq, GROUP_WIDTH), slab(0)),
                  pl.BlockSpec((1, 1, tq, GROUP_WIDTH), slab(1)),
                  pl.BlockSpec((1, 1, tq, GROUP_WIDTH), slab(2)),
                  pl.BlockSpec((1, 1, BAND, GROUP_WIDTH), prev_slab(1)),
                  pl.BlockSpec((1, 1, BAND, GROUP_WIDTH), prev_slab(2)),
                  pl.BlockSpec((HEADS_PER_GROUP, BAND, 2 * BAND), lambda b, r, i: (group, 0, 0))],
        out_specs=[pl.BlockSpec((1, 1, tq, GROUP_WIDTH), lambda b, r, i: (b, r, i, 0)),
                   pl.BlockSpec((1, 1, tq, LANES), lambda b, r, i: (b, r, i, 0))],
        out_shape=[jax.ShapeDtypeStruct((batch, d, u, GROUP_WIDTH), BF16),
                   jax.ShapeDtypeStruct((batch, d, u, LANES), F32)],
        scratch_shapes=[pltpu.VMEM((BAND + tq, GROUP_WIDTH), BF16),
                        pltpu.VMEM((BAND + tq, 2 * GROUP_WIDTH), BF16)],
        compiler_params=_params("parallel", "parallel", "parallel"),
        name=f"attn_group{group}",
    )(qkv, qkv, qkv, qkv, qkv, bias)


def _stream_to_token_permutation(tm, d):
    n = tm // d
    t = np.arange(tm)
    perm = np.zeros((tm, tm), np.float32)
    perm[t, (t % d) * n + t // d] = 1.0
    return perm


def _combine_kernel(x_ref, o0_ref, o1_ref, o2_ref, l0_ref, l1_ref, l2_ref, p1_ref, p2_ref,
                    w_ref, g_ref, out_ref, lt_ref):
    tm = x_ref.shape[1]
    o_tok = []
    for gi, (o_ref, l_ref, p_ref) in enumerate(((o1_ref, l1_ref, p1_ref), (o2_ref, l2_ref, p2_ref))):
        d = DILATIONS[gi + 1]
        n = tm // d
        for r in range(d):
            lt_ref[gi, pl.ds(r, n, stride=d), :] = l_ref[0, r]
        o_tok.append(_dot(p_ref[...], o_ref[0].reshape(tm, GROUP_WIDTH)))
    lses = [l0_ref[0, 0], lt_ref[0], lt_ref[1]]
    mx = jnp.maximum(jnp.maximum(lses[0], lses[1]), lses[2])
    es = [jnp.exp2(l - mx) for l in lses]
    inv = 1.0 / (es[0] + es[1] + es[2])
    alphas = [e * inv for e in es[1:]]
    parts = []
    for h in range(HEADS_PER_GROUP):
        cols = slice(h * HEAD_DIM, (h + 1) * HEAD_DIM)
        base = o0_ref[0, 0, :, cols].astype(F32)
        acc = base
        for gi in range(N_GROUPS - 1):
            acc = acc + alphas[gi][:, h:h + 1] * (o_tok[gi][:, cols] - base)
        parts.append(acc.astype(BF16))
    o = jnp.concatenate(parts, axis=1)
    out_ref[0] = x_ref[0] + _rms(_dot(o, w_ref[...]), g_ref[...])


def _combine_project(x, outs, lses, w_o, g_post):
    batch, seq, d_model = x.shape
    tm = min(ROW_TILE, seq)
    tile = pl.BlockSpec((1, tm, d_model), lambda b, i: (b, i, 0))
    stream = lambda d, width: pl.BlockSpec((1, d, tm // d, width), lambda b, i: (b, 0, i, 0))
    perms = [jnp.asarray(_stream_to_token_permutation(tm, d), BF16) for d in DILATIONS[1:]]
    return pl.pallas_call(
        _combine_kernel,
        grid=(batch, seq // tm),
        in_specs=[tile] + [stream(d, GROUP_WIDTH) for d in DILATIONS]
                 + [stream(d, LANES) for d in DILATIONS]
                 + [_resident(p.shape) for p in perms]
                 + [_resident(w_o.shape), _resident((1, d_model))],
        out_specs=tile,
        out_shape=jax.ShapeDtypeStruct(x.shape, F32),
        scratch_shapes=[pltpu.VMEM((N_GROUPS - 1, tm, LANES), F32)],
        compiler_params=_params("parallel", "parallel"),
        name="attn_combine_out",
    )(x, *outs, *lses, *perms, w_o, g_post.reshape(1, d_model))


def _attention_layer(x, g_pre, g_post, w_qkv, w_o, bias):
    qkvs = _qkv_proj(x, g_pre, w_qkv.astype(BF16))
    outs, lses = [], []
    for group in range(N_GROUPS):
        o, lse = _group_attention(qkvs[group], bias, group)
        outs.append(o)
        lses.append(lse)
    return _combine_project(x, outs, lses, w_o.astype(BF16), g_post)


SEGMENTS = SUBLANES
SEG_PITCH_PAD = 8


def _to_interleaved(h, stage_ref, dst_ref):
    tm, d = h.shape
    seg = tm // SEGMENTS
    pitch = seg + SEG_PITCH_PAD
    for s in range(SEGMENTS):
        for j in range(d // LANES):
            stage_ref[j, s * pitch:s * pitch + seg, :] = h[s * seg:(s + 1) * seg, j * LANES:(j + 1) * LANES]
    for i in range(0, seg, 2):
        for j in range(d // LANES):
            pair = [stage_ref[j, pl.ds(i + e, SEGMENTS, stride=pitch), :] for e in range(2)]
            dst_ref[i * SEGMENTS:(i + 2) * SEGMENTS, j * LANES:(j + 1) * LANES] = (
                jnp.concatenate(pair, axis=0).astype(dst_ref.dtype))


def _residual_from_interleaved(y, stage_ref, x_ref, out_ref):
    tm, d = y.shape
    seg = tm // SEGMENTS
    pitch = seg + SEG_PITCH_PAD
    for i in range(seg):
        for j in range(d // LANES):
            stage_ref[j, pl.ds(i, SEGMENTS, stride=pitch), :] = (
                y[i * SEGMENTS:(i + 1) * SEGMENTS, j * LANES:(j + 1) * LANES])
    for s in range(SEGMENTS):
        rows = slice(s * seg, (s + 1) * seg)
        for j in range(d // LANES):
            cols = slice(j * LANES, (j + 1) * LANES)
            out_ref[rows, cols] = x_ref[rows, cols] + stage_ref[j, s * pitch:s * pitch + seg, :]


def _stage_conv_input(u, dst_ref, dst_cols, halo_ref, halo_cols, taps):
    tm, width = u.shape
    seg = tm // SEGMENTS
    head = (taps - 1) * SEGMENTS
    row = lax.broadcasted_iota(jnp.int32, (SEGMENTS, width), 0)
    dst_ref[head:head + tm, dst_cols] = u
    for k in range(1, taps):
        grp = u[(seg - k) * SEGMENTS:(seg - k + 1) * SEGMENTS, :]
        before = jnp.where(row == 0, halo_ref[k - 1:k, halo_cols], pltpu.roll(grp, 1, 0))
        dst_ref[head - k * SEGMENTS:head - (k - 1) * SEGMENTS, dst_cols] = before
        halo_ref[k - 1:k, halo_cols] = grp[SEGMENTS - 1:SEGMENTS, :]


def _causal_conv(src_ref, src_cols, tm, w_ref, w_cols, bias):
    taps = w_ref.shape[0]
    out = bias
    for t in range(taps):
        start = t * SEGMENTS
        out = out + src_ref[start:start + tm, src_cols] * w_ref[t:t + 1, w_cols]
    return out


def _log_sigmoid(x):
    return jnp.minimum(x, 0.0) - jnp.log1p(jnp.exp(-jnp.abs(x)))


def _sublane_scan(q, e):
    row = lax.broadcasted_iota(jnp.int32, q.shape, 0)
    for s in (1, 2, 4):
        q_prev = jnp.where(row >= s, pltpu.roll(q, s, 0), 1.0)
        e_prev = jnp.where(row >= s, pltpu.roll(e, s, 0), 0.0)
        e = q * e_prev + e
        q = q * q_prev
    return e


def _lru_kernel(x_ref, gpre_ref, gpost_ref, win_ref, cw_ref, cb_ref, gaw_ref, gab_ref,
                gxw_ref, gxb_ref, lam_ref, wout_ref, out_ref,
                halo_ref, stage_ref, hp_ref, xs_ref, a_ref, b_ref, p_ref, h_ref):
    n_sub, tm, width = a_ref.shape
    seg = tm // SEGMENTS
    taps = cw_ref.shape[0]

    @pl.when(pl.program_id(1) == 0)
    def _():
        halo_ref[...] = jnp.zeros_like(halo_ref)
        h_ref[...] = jnp.zeros_like(h_ref)

    blk = width // LRU_BLOCKS
    row = lax.broadcasted_iota(jnp.int32, (SEGMENTS, blk), 0)
    log2_decay = (LRU_C * LOG2E) * _log_sigmoid(lam_ref[...])
    block_cols = [slice(n * blk, (n + 1) * blk) for n in range(LRU_BLOCKS)]
    items = [(t, n) for t in range(n_sub) for n in range(LRU_BLOCKS)]
    vals = {item: dict() for item in items}
    hn = {}
    y_parts = {t: [] for t in range(n_sub)}

    def head(t):
        rows = slice(t * tm, (t + 1) * tm)
        _to_interleaved(_rms(x_ref[0, rows, :], gpre_ref[...]), stage_ref.at[t], hp_ref.at[t])
        hn[t] = hp_ref[t]

    def tail(t):
        rows = slice(t * tm, (t + 1) * tm)
        y = functools.reduce(lambda acc, part: acc + part, y_parts[t])
        _residual_from_interleaved(_rms(y, gpost_ref[...]), stage_ref.at[t],
                                   x_ref.at[0, rows], out_ref.at[0, rows])

    def project(item):
        t, n = item
        vals[item]["u"] = _dot(hn[t], win_ref[:, block_cols[n]])
        vals[item]["g"] = _dot(hn[t], win_ref[:, width + n * blk:width + (n + 1) * blk])

    def conv(item):
        t, n = item
        cols = block_cols[n]
        _stage_conv_input(vals[item].pop("u"), xs_ref.at[t], cols, halo_ref, cols, taps)
        vals[item]["xc"] = _causal_conv(xs_ref.at[t], cols, tm, cw_ref, cols, cb_ref[:, cols])
        vals[item]["gate"] = _gelu_tanh(vals[item].pop("g"))

    def gates(item):
        n = item[1]
        xb = vals[item]["xc"].astype(BF16)
        vals[item]["r"] = _dot(xb, gaw_ref[n])
        vals[item]["i"] = _dot(xb, gxw_ref[n])

    def decay(item):
        t, n = item
        cols = block_cols[n]
        r = _sigmoid(vals[item].pop("r") + gab_ref[n:n + 1, :])
        gi = _sigmoid(vals[item].pop("i") + gxb_ref[n:n + 1, :])
        a = jnp.exp2(r * log2_decay[:, cols])
        a_ref[t, :, cols] = a
        b_ref[t, :, cols] = (jnp.exp2(0.5 * jnp.log2(jnp.maximum(1.0 - a * a, 0.0)))
                             * gi * vals[item].pop("xc"))

    def scan(item):
        t, n = item
        cols = block_cols[n]
        h = jnp.zeros((SEGMENTS, blk), F32)
        p = jnp.ones((SEGMENTS, blk), F32)
        for i in range(seg):
            rows = slice(i * SEGMENTS, (i + 1) * SEGMENTS)
            a_i = a_ref[t, rows, cols]
            h = a_i * h + b_ref[t, rows, cols]
            p = a_i * p
            b_ref[t, rows, cols] = h
            p_ref[t, rows, cols] = p
        e = jnp.where(row == 0, h_ref[:, cols], pltpu.roll(h, 1, 0))
        q = jnp.where(row == 0, 0.0, pltpu.roll(p, 1, 0))
        entry = _sublane_scan(q, e)
        h_ref[:, cols] = (h + p * entry)[SEGMENTS - 1:SEGMENTS, :]
        vals[item]["entry"] = entry

    def project_out(item):
        t, n = item
        cols = block_cols[n]
        hs = (b_ref[t, :, cols].reshape(seg, SEGMENTS, blk)
              + p_ref[t, :, cols].reshape(seg, SEGMENTS, blk) * vals[item].pop("entry")[None]
              ).reshape(tm, blk)
        y_parts[t].append(_dot((hs * vals[item].pop("gate")).astype(BF16), wout_ref[cols, :]))

    stages = (project, conv, gates, decay, scan, project_out)
    head(0)
    for slot in range(len(items) + len(stages) - 1):
        for k, stage in enumerate(stages):
            if 0 <= slot - k < len(items):
                item = items[slot - k]
                stage(item)
                if stage is project_out and item[1] == LRU_BLOCKS - 1:
                    tail(item[0])
        if slot == 0:
            for t in range(1, n_sub):
                head(t)


def _interleave_scratch(tm, d):
    seg = tm // SEGMENTS
    return [pltpu.VMEM((d // LANES, SEGMENTS * (seg + SEG_PITCH_PAD), LANES), F32),
            pltpu.VMEM((tm, d), BF16)]


def _lru_layer(x, g_pre, g_post, w_in, conv_w, conv_b, ga_w, ga_b, gx_w, gx_b, lam, w_out):
    batch, seq, d = x.shape
    width = w_out.shape[0]
    taps = conv_w.shape[0]
    tm = min(ROW_TILE, seq)
    n_sub = 2 if seq % (2 * tm) == 0 else 1
    seg = tm // SEGMENTS
    tile = pl.BlockSpec((1, n_sub * tm, d), lambda b, i: (b, i, 0))
    ops = [g_pre.reshape(1, d), g_post.reshape(1, d), w_in.astype(BF16), conv_w,
           conv_b.reshape(1, width), ga_w.astype(BF16), ga_b, gx_w.astype(BF16), gx_b,
           lam.reshape(1, width), w_out.astype(BF16)]
    return pl.pallas_call(
        _lru_kernel,
        grid=(batch, seq // (n_sub * tm)),
        in_specs=[tile] + [_resident(o.shape) for o in ops],
        out_specs=tile,
        out_shape=jax.ShapeDtypeStruct(x.shape, F32),
        scratch_shapes=[pltpu.VMEM((taps - 1, width), F32),
                        pltpu.VMEM((n_sub, d // LANES, SEGMENTS * (seg + SEG_PITCH_PAD), LANES), F32),
                        pltpu.VMEM((n_sub, tm, d), BF16),
                        pltpu.VMEM((n_sub, (taps - 1) * SEGMENTS + tm, width), F32),
                        pltpu.VMEM((n_sub, tm, width), F32),
                        pltpu.VMEM((n_sub, tm, width), F32),
                        pltpu.VMEM((n_sub, tm, width), F32),
                        pltpu.VMEM((1, width), F32)],
        compiler_params=_params("parallel", "arbitrary"),
        name="rglru_layer",
    )(x, *ops)


def _ffn_kernel(x_ref, gpre_ref, gpost_ref, wup_ref, cw_ref, cb_ref, wdn_ref, out_ref,
                halo_ref, stage_ref, hp_ref, us_a0, us_a1, us_b0, us_b1, acc_ref, *, f_chunk):
    us_refs = ((us_a0, us_a1), (us_b0, us_b1))
    tm = acc_ref.shape[0]
    d_ff = wdn_ref.shape[0]
    taps = cw_ref.shape[0]

    @pl.when(pl.program_id(1) == 0)
    def _():
        halo_ref[...] = jnp.zeros_like(halo_ref)

    _to_interleaved(_rms(x_ref[0], gpre_ref[...]), stage_ref, hp_ref)
    hn = hp_ref[...]
    for c in range(d_ff // f_chunk):
        halves = []
        for part, off in enumerate((c * f_chunk, d_ff + c * f_chunk)):
            src = slice(off, off + f_chunk)
            us_ref = us_refs[c % 2][part]
            _stage_conv_input(_dot(hn, wup_ref[:, src]), us_ref, slice(None), halo_ref, src, taps)
            halves.append(_causal_conv(us_ref, slice(None), tm, cw_ref, src, cb_ref[:, src]))
        act = (_gelu_tanh(halves[0]) * halves[1]).astype(BF16)
        contrib = _dot(act, wdn_ref[c * f_chunk:(c + 1) * f_chunk, :])
        if c == 0:
            acc_ref[...] = contrib
        else:
            acc_ref[...] += contrib
    _residual_from_interleaved(_rms(acc_ref[...], gpost_ref[...]), stage_ref,
                               x_ref.at[0], out_ref.at[0])


def _ffn_layer(x, g_pre, g_post, w_up, conv_w, conv_b, w_down):
    batch, seq, d = x.shape
    d_ff = w_down.shape[0]
    taps = conv_w.shape[0]
    f_chunk = 1024
    tm = min(ROW_TILE, seq)
    tile = pl.BlockSpec((1, tm, d), lambda b, i: (b, i, 0))
    ops = [g_pre.reshape(1, d), g_post.reshape(1, d), w_up.astype(BF16), conv_w,
           conv_b.reshape(1, 2 * d_ff), w_down.astype(BF16)]
    return pl.pallas_call(
        functools.partial(_ffn_kernel, f_chunk=f_chunk),
        grid=(batch, seq // tm),
        in_specs=[tile] + [_resident(o.shape) for o in ops],
        out_specs=tile,
        out_shape=jax.ShapeDtypeStruct(x.shape, F32),
        scratch_shapes=[pltpu.VMEM((taps - 1, 2 * d_ff), F32)] + _interleave_scratch(tm, d)
                       + [pltpu.VMEM(((taps - 1) * SEGMENTS + tm, f_chunk), F32)] * 4
                       + [pltpu.VMEM((tm, d), F32)],
        compiler_params=_params("parallel", "arbitrary"),
        name="conv_ffn",
    )(x, *ops)


def kernel(x, norm_mix_pre, norm_mix_post, norm_ffn_pre, norm_ffn_post, rel_bias, attn_w_qkv, attn_w_o, lru_w_in, lru_conv_w, lru_conv_b, lru_ga_w, lru_ga_b, lru_gx_w, lru_gx_b, lru_lambda, lru_w_out, ffn_w_up, ffn_conv_w, ffn_conv_b, ffn_w_down):
    bias = _bias_tables(rel_bias)
    for layer in range(norm_mix_pre.shape[0]):
        j = layer // 2
        if layer % 2 == 0:
            x = _attention_layer(x, norm_mix_pre[layer], norm_mix_post[layer],
                                 attn_w_qkv[j], attn_w_o[j], bias)
        else:
            x = _lru_layer(x, norm_mix_pre[layer], norm_mix_post[layer], lru_w_in[j],
                           lru_conv_w[j], lru_conv_b[j], lru_ga_w[j], lru_ga_b[j],
                           lru_gx_w[j], lru_gx_b[j], lru_lambda[j], lru_w_out[j])
        x = _ffn_layer(x, norm_ffn_pre[layer], norm_ffn_post[layer], ffn_w_up[layer],
                       ffn_conv_w[layer], ffn_conv_b[layer], ffn_w_down[layer])
    return x
```

```python
import functools

import numpy as np
import jax
import jax.numpy as jnp
from jax import lax
from jax.experimental import pallas as pl
from jax.experimental.pallas import tpu as pltpu

F32 = jnp.float32
BF16 = jnp.bfloat16

RMS_EPS = 1e-6
NEG_INF = -1e30
HEAD_DIM = 128
HEADS_PER_GROUP = 8
WINDOWS = (128, 512, 2048)
DILATIONS = (1, 4, 16)
N_GROUPS = 3
BAND = 128
NUM_BUCKETS = 32
MAX_DISTANCE = 2048
LRU_C = 8.0
LOG2E = float(np.log2(np.e))
Q_SCALE = HEAD_DIM ** -0.5 * LOG2E
LRU_BLOCKS = 4
SUBLANES = 8
LANES = 128
VMEM_LIMIT_BYTES = 56 * 1024 * 1024
ROW_TILE = 512
ATTN_TILE = 1024


def _params(*semantics):
    return pltpu.CompilerParams(dimension_semantics=semantics,
                                vmem_limit_bytes=VMEM_LIMIT_BYTES)


def _resident(shape):
    nd = len(shape)
    return pl.BlockSpec(shape, lambda *_: (0,) * nd, pipeline_mode=pl.Buffered(1))


def _rms(x, g):
    return x * lax.rsqrt(jnp.mean(x * x, axis=-1, keepdims=True) + RMS_EPS) * g


def _dot(a, b):
    return jnp.dot(a, b, preferred_element_type=F32)


def _gelu_tanh(x):
    c = float(np.sqrt(2.0 / np.pi))
    t = jnp.tanh(x * (c + (c * 0.044715) * (x * x)))
    return x * (0.5 + 0.5 * t)


def _sigmoid(x):
    return 0.5 + 0.5 * jnp.tanh(0.5 * x)


def _t5_bucket(dist):
    max_exact = NUM_BUCKETS // 2
    d = np.maximum(dist, 1).astype(np.float64)
    large = max_exact + (np.log(d / max_exact) / np.log(MAX_DISTANCE / max_exact)
                         * (NUM_BUCKETS - max_exact)).astype(np.int32)
    large = np.minimum(large, NUM_BUCKETS - 1)
    return np.where(dist < max_exact, dist, large).astype(np.int32)


def _bucket_tables():
    i = np.arange(BAND)[:, None]
    k = np.arange(2 * BAND)[None, :]
    m = i + BAND - k
    valid = (m >= 0) & (m <= BAND)
    tabs = [np.where(valid, _t5_bucket(np.clip(m, 0, BAND) * d), -1) for d in DILATIONS]
    return np.stack(tabs).astype(np.int32)


def _bias_kernel(tbl_ref, bucket_ref, out_ref):
    h = pl.program_id(0)
    bucket = bucket_ref[0]
    acc = jnp.where(bucket < 0, NEG_INF, 0.0).astype(F32)
    for j in range(NUM_BUCKETS):
        acc = jnp.where(bucket == j, tbl_ref[j, h], acc)
    out_ref[0] = acc * LOG2E


def _bias_tables(rel_bias):
    n_heads = rel_bias.shape[1]
    buckets = jnp.asarray(_bucket_tables())
    return pl.pallas_call(
        _bias_kernel,
        grid=(n_heads,),
        in_specs=[pl.BlockSpec(memory_space=pltpu.SMEM),
                  pl.BlockSpec((1, BAND, 2 * BAND), lambda h: (h // HEADS_PER_GROUP, 0, 0))],
        out_specs=pl.BlockSpec((1, BAND, 2 * BAND), lambda h: (h, 0, 0)),
        out_shape=jax.ShapeDtypeStruct((n_heads, BAND, 2 * BAND), F32),
        compiler_params=_params("arbitrary"),
        name="bias_tables",
    )(rel_bias.astype(F32), buckets)


GROUP_WIDTH = HEADS_PER_GROUP * HEAD_DIM
N_SLABS = GROUP_WIDTH // LANES


def _qkv_kernel(x_ref, g_ref, w_ref, o0_ref, o1_ref, o2_ref, hs_ref, hp_ref):
    tm = x_ref.shape[1]
    h = _rms(x_ref[0], g_ref[...])
    hb = h.astype(BF16)
    for c in range(3):
        cols = slice(c * GROUP_WIDTH, (c + 1) * GROUP_WIDTH)
        res = _dot(hb, w_ref[:, cols])
        o0_ref[0, 0, :, cols] = (res * Q_SCALE if c == 0 else res).astype(o0_ref.dtype)
    for j in range(N_SLABS):
        hs_ref[j] = h[:, j * LANES:(j + 1) * LANES]
    for group, o_ref in ((1, o1_ref), (2, o2_ref)):
        d = DILATIONS[group]
        n = tm // d
        for r in range(d):
            for j in range(N_SLABS):
                hp_ref[r * n:(r + 1) * n, j * LANES:(j + 1) * LANES] = (
                    hs_ref[j, pl.ds(r, n, stride=d), :].astype(BF16))
        hp = hp_ref[...]
        for c in range(3):
            cols = slice(c * GROUP_WIDTH, (c + 1) * GROUP_WIDTH)
            wcols = slice((3 * group + c) * GROUP_WIDTH, (3 * group + c + 1) * GROUP_WIDTH)
            res = _dot(hp, w_ref[:, wcols])
            res = (res * Q_SCALE if c == 0 else res).astype(o_ref.dtype)
            for r in range(d):
                o_ref[0, r, :, cols] = res[r * n:(r + 1) * n, :]


def _qkv_proj(x, g, w):
    batch, seq, d_model = x.shape
    tm = min(ROW_TILE, seq)
    out_specs, out_shape = [], []
    for d in DILATIONS:
        out_specs.append(pl.BlockSpec((1, d, tm // d, 3 * GROUP_WIDTH), lambda b, i: (b, 0, i, 0)))
        out_shape.append(jax.ShapeDtypeStruct((batch, d, seq // d, 3 * GROUP_WIDTH), BF16))
    return pl.pallas_call(
        _qkv_kernel,
        grid=(batch, seq // tm),
        in_specs=[pl.BlockSpec((1, tm, d_model), lambda b, i: (b, i, 0)),
                  _resident((1, d_model)), _resident(w.shape)],
        out_specs=out_specs,
        out_shape=out_shape,
        scratch_shapes=[pltpu.VMEM((N_SLABS, tm, LANES), F32),
                        pltpu.VMEM((tm, d_model), BF16)],
        compiler_params=_params("parallel", "parallel"),
        name="qkv_proj",
    )(x, g.reshape(1, d_model), w)


def _attn_kernel(q_ref, k_ref, v_ref, kp_ref, vp_ref, bias_ref, o_ref, lse_ref, kk_ref, vv_ref):
    n_streams, tq = q_ref.shape[1], q_ref.shape[2]
    first_tile = pl.program_id(2) == 0
    ones = jnp.ones((BAND + tq, HEAD_DIM), BF16)
    lane = lax.broadcasted_iota(jnp.int32, (BAND, LANES), 1)
    key_col = lax.broadcasted_iota(jnp.int32, (BAND, 2 * BAND), 1)
    for st in range(n_streams):
        kk_ref[st, 0:BAND, :] = kp_ref[0, st]
        kk_ref[st, BAND:, :] = k_ref[0, st]
        for h in range(HEADS_PER_GROUP):
            cols = slice(h * HEAD_DIM, (h + 1) * HEAD_DIM)
            vv_ref[st, 0:BAND, 2 * h * HEAD_DIM:(2 * h + 1) * HEAD_DIM] = vp_ref[0, st, :, cols]
            vv_ref[st, BAND:, 2 * h * HEAD_DIM:(2 * h + 1) * HEAD_DIM] = v_ref[0, st, :, cols]
            vv_ref[st, :, (2 * h + 1) * HEAD_DIM:(2 * h + 2) * HEAD_DIM] = ones
        for j in range(tq // BAND):
            rows = slice(j * BAND, (j + 1) * BAND)
            win = slice(j * BAND, (j + 2) * BAND)
            lse_tile = jnp.zeros((BAND, LANES), F32)
            for h in range(HEADS_PER_GROUP):
                cols = slice(h * HEAD_DIM, (h + 1) * HEAD_DIM)
                logits = lax.dot_general(q_ref[0, st, rows, cols], kk_ref[st, win, cols],
                                         (((1,), (1,)), ((), ())), preferred_element_type=F32)
                logits = logits + bias_ref[h]
                if j == 0:
                    logits = jnp.where(first_tile & (key_col < BAND), NEG_INF, logits)
                mx = jnp.max(logits, axis=-1, keepdims=True)
                p = jnp.exp2(logits - mx)
                pv = _dot(p.astype(BF16),
                          vv_ref[st, win, 2 * h * HEAD_DIM:(2 * h + 2) * HEAD_DIM])
                s = pv[:, HEAD_DIM:]
                o_ref[0, st, rows, cols] = (pv[:, :HEAD_DIM] / s).astype(o_ref.dtype)
                lse_tile = jnp.where(lane == h, mx + jnp.log2(s), lse_tile)
            lse_ref[0, st, rows, :] = lse_tile


def _group_attention(qkv, bias, group):
    batch, d, u, _ = qkv.shape
    tq = min(ATTN_TILE, u)
    ns = min(d, ATTN_TILE // tq)
    nblk = tq // BAND

    def slab(which):
        return lambda b, r, i: (b, r, i, which)

    def prev_slab(which):
        return lambda b, r, i: (b, r, jnp.maximum(i * nblk - 1, 0), which)

    return pl.pallas_call(
        _attn_kernel,
        grid=(batch, d // ns, u // tq),
        in_specs=[pl.BlockSpec((1, ns, tq, GROUP_WIDTH), slab(0)),
                  pl.BlockSpec((1, ns, tq, GROUP_WIDTH), slab(1)),
                  pl.BlockSpec((1, ns, tq, GROUP_WIDTH), slab(2)),
                  pl.BlockSpec((1, ns, BAND, GROUP_WIDTH), prev_slab(1)),
                  pl.BlockSpec((1, ns, BAND, GROUP_WIDTH), prev_slab(2)),
                  pl.BlockSpec((HEADS_PER_GROUP, BAND, 2 * BAND), lambda b, r, i: (group, 0, 0))],
        out_specs=[pl.BlockSpec((1, ns, tq, GROUP_WIDTH), lambda b, r, i: (b, r, i, 0)),
                   pl.BlockSpec((1, ns, tq, LANES), lambda b, r, i: (b, r, i, 0))],
        out_shape=[jax.ShapeDtypeStruct((batch, d, u, GROUP_WIDTH), BF16),
                   jax.ShapeDtypeStruct((batch, d, u, LANES), F32)],
        scratch_shapes=[pltpu.VMEM((ns, BAND + tq, GROUP_WIDTH), BF16),
                        pltpu.VMEM((ns, BAND + tq, 2 * GROUP_WIDTH), BF16)],
        compiler_params=_params("parallel", "parallel", "parallel"),
        name=f"attn_group{group}",
    )(qkv, qkv, qkv, qkv, qkv, bias)


def _stream_to_token_permutation(tm, d):
    n = tm // d
    t = np.arange(tm)
    perm = np.zeros((tm, tm), np.float32)
    perm[t, (t % d) * n + t // d] = 1.0
    return perm


def _combine_tile(x_ref, o0_ref, o1_ref, o2_ref, l0_ref, l1_ref, l2_ref, p1_ref, p2_ref,
                  w_ref, g_ref, lt_ref):
    tm = x_ref.shape[0]
    o_tok = []
    for gi, (o_ref, l_ref, p_ref) in enumerate(((o1_ref, l1_ref, p1_ref), (o2_ref, l2_ref, p2_ref))):
        d = DILATIONS[gi + 1]
        n = tm // d
        for r in range(d):
            lt_ref[gi, pl.ds(r, n, stride=d), :] = l_ref[0, r]
        o_tok.append(_dot(p_ref[...], o_ref[0].reshape(tm, GROUP_WIDTH)))
    lses = [l0_ref[0, 0], lt_ref[0], lt_ref[1]]
    mx = jnp.maximum(jnp.maximum(lses[0], lses[1]), lses[2])
    es = [jnp.exp2(l - mx) for l in lses]
    inv = 1.0 / (es[0] + es[1] + es[2])
    alphas = [e * inv for e in es[1:]]
    parts = []
    for h in range(HEADS_PER_GROUP):
        cols = slice(h * HEAD_DIM, (h + 1) * HEAD_DIM)
        base = o0_ref[0, 0, :, cols].astype(F32)
        acc = base
        for gi in range(N_GROUPS - 1):
            acc = acc + alphas[gi][:, h:h + 1] * (o_tok[gi][:, cols] - base)
        parts.append(acc.astype(BF16))
    o = jnp.concatenate(parts, axis=1)
    return x_ref[...] + _rms(_dot(o, w_ref[...]), g_ref[...])


def _combine_kernel(x_ref, o0_ref, o1_ref, o2_ref, l0_ref, l1_ref, l2_ref, p1_ref, p2_ref,
                    w_ref, g_ref, out_ref, lt_ref):
    out_ref[0] = _combine_tile(x_ref.at[0], o0_ref, o1_ref, o2_ref, l0_ref, l1_ref, l2_ref,
                               p1_ref, p2_ref, w_ref, g_ref, lt_ref)


def _combine_project(x, outs, lses, w_o, g_post):
    batch, seq, d_model = x.shape
    tm = min(ROW_TILE, seq)
    tile = pl.BlockSpec((1, tm, d_model), lambda b, i: (b, i, 0))
    stream = lambda d, width: pl.BlockSpec((1, d, tm // d, width), lambda b, i: (b, 0, i, 0))
    perms = [jnp.asarray(_stream_to_token_permutation(tm, d), BF16) for d in DILATIONS[1:]]
    return pl.pallas_call(
        _combine_kernel,
        grid=(batch, seq // tm),
        in_specs=[tile] + [stream(d, GROUP_WIDTH) for d in DILATIONS]
                 + [stream(d, LANES) for d in DILATIONS]
                 + [_resident(p.shape) for p in perms]
                 + [_resident(w_o.shape), _resident((1, d_model))],
        out_specs=tile,
        out_shape=jax.ShapeDtypeStruct(x.shape, F32),
        scratch_shapes=[pltpu.VMEM((N_GROUPS - 1, tm, LANES), F32)],
        compiler_params=_params("parallel", "parallel"),
        name="attn_combine_out",
    )(x, *outs, *lses, *perms, w_o, g_post.reshape(1, d_model))


def _attention_layer(x, g_pre, g_post, w_qkv, w_o, bias):
    qkvs = _qkv_proj(x, g_pre, w_qkv.astype(BF16))
    outs, lses = [], []
    for group in range(N_GROUPS):
        o, lse = _group_attention(qkvs[group], bias, group)
        outs.append(o)
        lses.append(lse)
    return _combine_project(x, outs, lses, w_o.astype(BF16), g_post)


SEGMENTS = SUBLANES
SEG_PITCH_PAD = 8


def _to_interleaved(h, stage_ref, dst_ref):
    tm, d = h.shape
    seg = tm // SEGMENTS
    pitch = seg + SEG_PITCH_PAD
    for s in range(SEGMENTS):
        for j in range(d // LANES):
            stage_ref[j, s * pitch:s * pitch + seg, :] = h[s * seg:(s + 1) * seg, j * LANES:(j + 1) * LANES]
    for i in range(0, seg, 2):
        for j in range(d // LANES):
            pair = [stage_ref[j, pl.ds(i + e, SEGMENTS, stride=pitch), :] for e in range(2)]
            dst_ref[i * SEGMENTS:(i + 2) * SEGMENTS, j * LANES:(j + 1) * LANES] = (
                jnp.concatenate(pair, axis=0).astype(dst_ref.dtype))


def _residual_from_interleaved(y, stage_ref, x_ref, out_ref):
    tm, d = y.shape
    seg = tm // SEGMENTS
    pitch = seg + SEG_PITCH_PAD
    for i in range(seg):
        for j in range(d // LANES):
            stage_ref[j, pl.ds(i, SEGMENTS, stride=pitch), :] = (
                y[i * SEGMENTS:(i + 1) * SEGMENTS, j * LANES:(j + 1) * LANES])
    for s in range(SEGMENTS):
        rows = slice(s * seg, (s + 1) * seg)
        for j in range(d // LANES):
            cols = slice(j * LANES, (j + 1) * LANES)
            out_ref[rows, cols] = x_ref[rows, cols] + stage_ref[j, s * pitch:s * pitch + seg, :]


def _stage_conv_input(u, dst_ref, dst_cols, halo_ref, halo_cols, taps):
    tm, width = u.shape
    seg = tm // SEGMENTS
    head = (taps - 1) * SEGMENTS
    row = lax.broadcasted_iota(jnp.int32, (SEGMENTS, width), 0)
    dst_ref[head:head + tm, dst_cols] = u
    for k in range(1, taps):
        grp = u[(seg - k) * SEGMENTS:(seg - k + 1) * SEGMENTS, :]
        before = jnp.where(row == 0, halo_ref[k - 1:k, halo_cols], pltpu.roll(grp, 1, 0))
        dst_ref[head - k * SEGMENTS:head - (k - 1) * SEGMENTS, dst_cols] = before
        halo_ref[k - 1:k, halo_cols] = grp[SEGMENTS - 1:SEGMENTS, :]


def _causal_conv(src_ref, src_cols, tm, w_ref, w_cols, bias):
    taps = w_ref.shape[0]
    out = bias
    for t in range(taps):
        start = t * SEGMENTS
        out = out + src_ref[start:start + tm, src_cols] * w_ref[t:t + 1, w_cols]
    return out


def _log_sigmoid(x):
    return jnp.minimum(x, 0.0) - jnp.log1p(jnp.exp(-jnp.abs(x)))


def _sublane_scan(q, e):
    row = lax.broadcasted_iota(jnp.int32, q.shape, 0)
    for s in (1, 2, 4):
        q_prev = jnp.where(row >= s, pltpu.roll(q, s, 0), 1.0)
        e_prev = jnp.where(row >= s, pltpu.roll(e, s, 0), 0.0)
        e = q * e_prev + e
        q = q * q_prev
    return e


def _lru_kernel(x_ref, gpre_ref, gpost_ref, win_ref, cw_ref, cb_ref, gaw_ref, gab_ref,
                gxw_ref, gxb_ref, lam_ref, wout_ref, out_ref,
                halo_ref, stage_ref, hp_ref, xs_ref, a_ref, b_ref, p_ref, h_ref):
    n_sub, tm, width = a_ref.shape
    seg = tm // SEGMENTS
    taps = cw_ref.shape[0]

    @pl.when(pl.program_id(1) == 0)
    def _():
        halo_ref[...] = jnp.zeros_like(halo_ref)
        h_ref[...] = jnp.zeros_like(h_ref)

    blk = width // LRU_BLOCKS
    row = lax.broadcasted_iota(jnp.int32, (SEGMENTS, blk), 0)
    log2_decay = (LRU_C * LOG2E) * _log_sigmoid(lam_ref[...])
    block_cols = [slice(n * blk, (n + 1) * blk) for n in range(LRU_BLOCKS)]
    items = [(t, n) for t in range(n_sub) for n in range(LRU_BLOCKS)]
    vals = {item: dict() for item in items}
    hn = {}
    y_parts = {t: [] for t in range(n_sub)}

    def head(t):
        rows = slice(t * tm, (t + 1) * tm)
        _to_interleaved(_rms(x_ref[0, rows, :], gpre_ref[...]), stage_ref.at[t], hp_ref.at[t])
        hn[t] = hp_ref[t]

    def tail(t):
        rows = slice(t * tm, (t + 1) * tm)
        y = functools.reduce(lambda acc, part: acc + part, y_parts[t])
        _residual_from_interleaved(_rms(y, gpost_ref[...]), stage_ref.at[t],
                                   x_ref.at[0, rows], out_ref.at[0, rows])

    def project(item):
        t, n = item
        vals[item]["u"] = _dot(hn[t], win_ref[:, block_cols[n]])
        vals[item]["g"] = _dot(hn[t], win_ref[:, width + n * blk:width + (n + 1) * blk])

    def conv(item):
        t, n = item
        cols = block_cols[n]
        _stage_conv_input(vals[item].pop("u"), xs_ref.at[t], cols, halo_ref, cols, taps)
        vals[item]["xc"] = _causal_conv(xs_ref.at[t], cols, tm, cw_ref, cols, cb_ref[:, cols])
        vals[item]["gate"] = _gelu_tanh(vals[item].pop("g"))

    def gates(item):
        n = item[1]
        xb = vals[item]["xc"].astype(BF16)
        vals[item]["r"] = _dot(xb, gaw_ref[n])
        vals[item]["i"] = _dot(xb, gxw_ref[n])

    def decay(item):
        t, n = item
        cols = block_cols[n]
        r = _sigmoid(vals[item].pop("r") + gab_ref[n:n + 1, :])
        gi = _sigmoid(vals[item].pop("i") + gxb_ref[n:n + 1, :])
        a = jnp.exp2(r * log2_decay[:, cols])
        a_ref[t, :, cols] = a
        b_ref[t, :, cols] = (jnp.exp2(0.5 * jnp.log2(jnp.maximum(1.0 - a * a, 0.0)))
                             * gi * vals[item].pop("xc"))

    def scan(item):
        t, n = item
        cols = block_cols[n]
        h = jnp.zeros((SEGMENTS, blk), F32)
        p = jnp.ones((SEGMENTS, blk), F32)
        for i in range(seg):
            rows = slice(i * SEGMENTS, (i + 1) * SEGMENTS)
            a_i = a_ref[t, rows, cols]
            h = a_i * h + b_ref[t, rows, cols]
            p = a_i * p
            b_ref[t, rows, cols] = h
            p_ref[t, rows, cols] = p
        e = jnp.where(row == 0, h_ref[:, cols], pltpu.roll(h, 1, 0))
        q = jnp.where(row == 0, 0.0, pltpu.roll(p, 1, 0))
        entry = _sublane_scan(q, e)
        h_ref[:, cols] = (h + p * entry)[SEGMENTS - 1:SEGMENTS, :]
        vals[item]["entry"] = entry

    def project_out(item):
        t, n = item
        cols = block_cols[n]
        hs = (b_ref[t, :, cols].reshape(seg, SEGMENTS, blk)
              + p_ref[t, :, cols].reshape(seg, SEGMENTS, blk) * vals[item].pop("entry")[None]
              ).reshape(tm, blk)
        y_parts[t].append(_dot((hs * vals[item].pop("gate")).astype(BF16), wout_ref[cols, :]))

    stages = (project, conv, gates, decay, scan, project_out)
    head(0)
    for slot in range(len(items) + len(stages) - 1):
        for k, stage in enumerate(stages):
            if 0 <= slot - k < len(items):
                item = items[slot - k]
                stage(item)
                if stage is project_out and item[1] == LRU_BLOCKS - 1:
                    tail(item[0])
        if slot == 0:
            for t in range(1, n_sub):
                head(t)


def _interleave_scratch(tm, d):
    seg = tm // SEGMENTS
    return [pltpu.VMEM((d // LANES, SEGMENTS * (seg + SEG_PITCH_PAD), LANES), F32),
            pltpu.VMEM((tm, d), BF16)]


def _lru_layer(x, g_pre, g_post, w_in, conv_w, conv_b, ga_w, ga_b, gx_w, gx_b, lam, w_out):
    batch, seq, d = x.shape
    width = w_out.shape[0]
    taps = conv_w.shape[0]
    tm = min(ROW_TILE, seq)
    n_sub = 2 if seq % (2 * tm) == 0 else 1
    seg = tm // SEGMENTS
    tile = pl.BlockSpec((1, n_sub * tm, d), lambda b, i: (b, i, 0))
    ops = [g_pre.reshape(1, d), g_post.reshape(1, d), w_in.astype(BF16), conv_w,
           conv_b.reshape(1, width), ga_w.astype(BF16), ga_b, gx_w.astype(BF16), gx_b,
           lam.reshape(1, width), w_out.astype(BF16)]
    return pl.pallas_call(
        _lru_kernel,
        grid=(batch, seq // (n_sub * tm)),
        in_specs=[tile] + [_resident(o.shape) for o in ops],
        out_specs=tile,
        out_shape=jax.ShapeDtypeStruct(x.shape, F32),
        scratch_shapes=[pltpu.VMEM((taps - 1, width), F32),
                        pltpu.VMEM((n_sub, d // LANES, SEGMENTS * (seg + SEG_PITCH_PAD), LANES), F32),
                        pltpu.VMEM((n_sub, tm, d), BF16),
                        pltpu.VMEM((n_sub, (taps - 1) * SEGMENTS + tm, width), F32),
                        pltpu.VMEM((n_sub, tm, width), F32),
                        pltpu.VMEM((n_sub, tm, width), F32),
                        pltpu.VMEM((n_sub, tm, width), F32),
                        pltpu.VMEM((1, width), F32)],
        compiler_params=_params("parallel", "arbitrary"),
        name="rglru_layer",
    )(x, *ops)


FFN_CHUNK = 1024


def _ffn_tile(x_ref, out_ref, gpre_ref, gpost_ref, wup_ref, cw_ref, cb_ref, wdn_ref,
              halo_ref, stage_ref, hp_ref, us_g, us_v, acc_ref):
    tm = acc_ref.shape[0]
    d_ff = wdn_ref.shape[0]
    taps = cw_ref.shape[0]
    _to_interleaved(_rms(x_ref[...], gpre_ref[...]), stage_ref, hp_ref)
    hn = hp_ref[...]
    n_chunks = d_ff // FFN_CHUNK
    y = None
    for c in range(n_chunks):
        halves = []
        for part, off in enumerate((c * FFN_CHUNK, d_ff + c * FFN_CHUNK)):
            src = slice(off, off + FFN_CHUNK)
            us_ref = (us_g, us_v)[part]
            _stage_conv_input(_dot(hn, wup_ref[:, src]), us_ref, slice(None), halo_ref, src, taps)
            halves.append(_causal_conv(us_ref, slice(None), tm, cw_ref, src, cb_ref[:, src]))
        act = (_gelu_tanh(halves[0]) * halves[1]).astype(BF16)
        contrib = _dot(act, wdn_ref[c * FFN_CHUNK:(c + 1) * FFN_CHUNK, :])
        if c == n_chunks - 1:
            y = contrib if c == 0 else acc_ref[...] + contrib
        elif c == 0:
            acc_ref[...] = contrib
        else:
            acc_ref[...] += contrib
    _residual_from_interleaved(_rms(y, gpost_ref[...]), stage_ref, x_ref, out_ref)


def _ffn_scratch(tm, d, d_ff, taps):
    return ([pltpu.VMEM((taps - 1, 2 * d_ff), F32)] + _interleave_scratch(tm, d)
            + [pltpu.VMEM(((taps - 1) * SEGMENTS + tm, FFN_CHUNK), F32)] * 2
            + [pltpu.VMEM((tm, d), F32)])


def _ffn_operands(g_pre, g_post, w_up, conv_w, conv_b, w_down):
    d, d_ff = w_down.shape[1], w_down.shape[0]
    return [g_pre.reshape(1, d), g_post.reshape(1, d), w_up.astype(BF16), conv_w,
            conv_b.reshape(1, 2 * d_ff), w_down.astype(BF16)]


def _ffn_kernel(x_ref, gpre_ref, gpost_ref, wup_ref, cw_ref, cb_ref, wdn_ref, out_ref,
                halo_ref, *scratch):
    @pl.when(pl.program_id(1) == 0)
    def _():
        halo_ref[...] = jnp.zeros_like(halo_ref)

    _ffn_tile(x_ref.at[0], out_ref.at[0], gpre_ref, gpost_ref, wup_ref, cw_ref, cb_ref, wdn_ref,
              halo_ref, *scratch)


def _ffn_layer(x, g_pre, g_post, w_up, conv_w, conv_b, w_down):
    batch, seq, d = x.shape
    tm = min(ROW_TILE, seq)
    tile = pl.BlockSpec((1, tm, d), lambda b, i: (b, i, 0))
    ops = _ffn_operands(g_pre, g_post, w_up, conv_w, conv_b, w_down)
    return pl.pallas_call(
        _ffn_kernel,
        grid=(batch, seq // tm),
        in_specs=[tile] + [_resident(o.shape) for o in ops],
        out_specs=tile,
        out_shape=jax.ShapeDtypeStruct(x.shape, F32),
        scratch_shapes=_ffn_scratch(tm, d, w_down.shape[0], conv_w.shape[0]),
        compiler_params=_params("parallel", "arbitrary"),
        name="conv_ffn",
    )(x, *ops)


def kernel(x, norm_mix_pre, norm_mix_post, norm_ffn_pre, norm_ffn_post, rel_bias, attn_w_qkv, attn_w_o, lru_w_in, lru_conv_w, lru_conv_b, lru_ga_w, lru_ga_b, lru_gx_w, lru_gx_b, lru_lambda, lru_w_out, ffn_w_up, ffn_conv_w, ffn_conv_b, ffn_w_down):
    bias = _bias_tables(rel_bias)
    for layer in range(norm_mix_pre.shape[0]):
        j = layer // 2
        ffn = (norm_ffn_pre[layer], norm_ffn_post[layer], ffn_w_up[layer],
               ffn_conv_w[layer], ffn_conv_b[layer], ffn_w_down[layer])
        if layer % 2 == 0:
            x = _attention_layer(x, norm_mix_pre[layer], norm_mix_post[layer],
                                 attn_w_qkv[j], attn_w_o[j], bias)
        else:
            x = _lru_layer(x, norm_mix_pre[layer], norm_mix_post[layer], lru_w_in[j],
                           lru_conv_w[j], lru_conv_b[j], lru_ga_w[j], lru_ga_b[j],
                           lru_gx_w[j], lru_gx_b[j], lru_lambda[j], lru_w_out[j])
        x = _ffn_layer(x, *ffn)
    return x
```

```python
import functools

import numpy as np
import jax
import jax.numpy as jnp
from jax import lax
from jax.experimental import pallas as pl
from jax.experimental.pallas import tpu as pltpu

F32 = jnp.float32
BF16 = jnp.bfloat16

RMS_EPS = 1e-6
NEG_INF = -1e30
HEAD_DIM = 128
HEADS_PER_GROUP = 8
WINDOWS = (128, 512, 2048)
DILATIONS = (1, 4, 16)
N_GROUPS = 3
BAND = 128
NUM_BUCKETS = 32
MAX_DISTANCE = 2048
LRU_C = 8.0
LOG2E = float(np.log2(np.e))
Q_SCALE = HEAD_DIM ** -0.5 * LOG2E
LRU_BLOCKS = 4
SUBLANES = 8
LANES = 128
VMEM_LIMIT_BYTES = 56 * 1024 * 1024
ROW_TILE = 512
ATTN_TILE = 1024
LRU_SUB_TILE = 256
LRU_STEP_TOKENS = 1024


def _params(*semantics):
    return pltpu.CompilerParams(dimension_semantics=semantics,
                                vmem_limit_bytes=VMEM_LIMIT_BYTES)


def _resident(shape):
    nd = len(shape)
    return pl.BlockSpec(shape, lambda *_: (0,) * nd, pipeline_mode=pl.Buffered(1))


def _rms(x, g):
    return x * lax.rsqrt(jnp.mean(x * x, axis=-1, keepdims=True) + RMS_EPS) * g


def _dot(a, b):
    return jnp.dot(a, b, preferred_element_type=F32)


def _gelu_tanh(x):
    c = float(np.sqrt(2.0 / np.pi))
    t = jnp.tanh(x * (c + (c * 0.044715) * (x * x)))
    return x * (0.5 + 0.5 * t)


def _sigmoid(x):
    return 0.5 + 0.5 * jnp.tanh(0.5 * x)


def _t5_bucket(dist):
    max_exact = NUM_BUCKETS // 2
    d = np.maximum(dist, 1).astype(np.float64)
    large = max_exact + (np.log(d / max_exact) / np.log(MAX_DISTANCE / max_exact)
                         * (NUM_BUCKETS - max_exact)).astype(np.int32)
    large = np.minimum(large, NUM_BUCKETS - 1)
    return np.where(dist < max_exact, dist, large).astype(np.int32)


def _bucket_tables():
    i = np.arange(BAND)[:, None]
    k = np.arange(2 * BAND)[None, :]
    m = i + BAND - k
    valid = (m >= 0) & (m <= BAND)
    tabs = [np.where(valid, _t5_bucket(np.clip(m, 0, BAND) * d), -1) for d in DILATIONS]
    return np.stack(tabs).astype(np.int32)


def _bias_kernel(tbl_ref, bucket_ref, out_ref):
    h = pl.program_id(0)
    bucket = bucket_ref[0]
    acc = jnp.where(bucket < 0, NEG_INF, 0.0).astype(F32)
    for j in range(NUM_BUCKETS):
        acc = jnp.where(bucket == j, tbl_ref[j, h], acc)
    out_ref[0] = acc * LOG2E


def _bias_tables(rel_bias):
    n_heads = rel_bias.shape[1]
    buckets = jnp.asarray(_bucket_tables())
    return pl.pallas_call(
        _bias_kernel,
        grid=(n_heads,),
        in_specs=[pl.BlockSpec(memory_space=pltpu.SMEM),
                  pl.BlockSpec((1, BAND, 2 * BAND), lambda h: (h // HEADS_PER_GROUP, 0, 0))],
        out_specs=pl.BlockSpec((1, BAND, 2 * BAND), lambda h: (h, 0, 0)),
        out_shape=jax.ShapeDtypeStruct((n_heads, BAND, 2 * BAND), F32),
        compiler_params=_params("arbitrary"),
        name="bias_tables",
    )(rel_bias.astype(F32), buckets)


GROUP_WIDTH = HEADS_PER_GROUP * HEAD_DIM
N_SLABS = GROUP_WIDTH // LANES


def _qkv_kernel(x_ref, g_ref, w_ref, o0_ref, o1_ref, o2_ref, hs_ref, hp_ref):
    tm = x_ref.shape[1]
    h = _rms(x_ref[0], g_ref[...])
    hb = h.astype(BF16)
    for c in range(3):
        cols = slice(c * GROUP_WIDTH, (c + 1) * GROUP_WIDTH)
        res = _dot(hb, w_ref[:, cols])
        o0_ref[0, 0, :, cols] = (res * Q_SCALE if c == 0 else res).astype(o0_ref.dtype)
    for j in range(N_SLABS):
        hs_ref[j] = h[:, j * LANES:(j + 1) * LANES]
    for group, o_ref in ((1, o1_ref), (2, o2_ref)):
        d = DILATIONS[group]
        n = tm // d
        for r in range(d):
            for j in range(N_SLABS):
                hp_ref[r * n:(r + 1) * n, j * LANES:(j + 1) * LANES] = (
                    hs_ref[j, pl.ds(r, n, stride=d), :].astype(BF16))
        hp = hp_ref[...]
        for c in range(3):
            cols = slice(c * GROUP_WIDTH, (c + 1) * GROUP_WIDTH)
            wcols = slice((3 * group + c) * GROUP_WIDTH, (3 * group + c + 1) * GROUP_WIDTH)
            res = _dot(hp, w_ref[:, wcols])
            res = (res * Q_SCALE if c == 0 else res).astype(o_ref.dtype)
            for r in range(d):
                o_ref[0, r, :, cols] = res[r * n:(r + 1) * n, :]


def _qkv_proj(x, g, w):
    batch, seq, d_model = x.shape
    tm = min(ROW_TILE, seq)
    out_specs, out_shape = [], []
    for d in DILATIONS:
        out_specs.append(pl.BlockSpec((1, d, tm // d, 3 * GROUP_WIDTH), lambda b, i: (b, 0, i, 0)))
        out_shape.append(jax.ShapeDtypeStruct((batch, d, seq // d, 3 * GROUP_WIDTH), BF16))
    return pl.pallas_call(
        _qkv_kernel,
        grid=(batch, seq // tm),
        in_specs=[pl.BlockSpec((1, tm, d_model), lambda b, i: (b, i, 0)),
                  _resident((1, d_model)), _resident(w.shape)],
        out_specs=out_specs,
        out_shape=out_shape,
        scratch_shapes=[pltpu.VMEM((N_SLABS, tm, LANES), F32),
                        pltpu.VMEM((tm, d_model), BF16)],
        compiler_params=_params("parallel", "parallel"),
        name="qkv_proj",
    )(x, g.reshape(1, d_model), w)


def _attn_kernel(q_ref, k_ref, v_ref, kp_ref, vp_ref, bias_ref, o_ref, lse_ref, kk_ref, vv_ref):
    n_streams, tq = q_ref.shape[1], q_ref.shape[2]
    first_tile = pl.program_id(2) == 0
    ones = jnp.ones((BAND + tq, HEAD_DIM), BF16)
    lane = lax.broadcasted_iota(jnp.int32, (BAND, LANES), 1)
    key_col = lax.broadcasted_iota(jnp.int32, (BAND, 2 * BAND), 1)
    for st in range(n_streams):
        kk_ref[st, 0:BAND, :] = kp_ref[0, st]
        kk_ref[st, BAND:, :] = k_ref[0, st]
        for h in range(HEADS_PER_GROUP):
            cols = slice(h * HEAD_DIM, (h + 1) * HEAD_DIM)
            vv_ref[st, 0:BAND, 2 * h * HEAD_DIM:(2 * h + 1) * HEAD_DIM] = vp_ref[0, st, :, cols]
            vv_ref[st, BAND:, 2 * h * HEAD_DIM:(2 * h + 1) * HEAD_DIM] = v_ref[0, st, :, cols]
            vv_ref[st, :, (2 * h + 1) * HEAD_DIM:(2 * h + 2) * HEAD_DIM] = ones
        for j in range(tq // BAND):
            rows = slice(j * BAND, (j + 1) * BAND)
            win = slice(j * BAND, (j + 2) * BAND)
            lse_tile = jnp.zeros((BAND, LANES), F32)
            for h in range(HEADS_PER_GROUP):
                cols = slice(h * HEAD_DIM, (h + 1) * HEAD_DIM)
                logits = lax.dot_general(q_ref[0, st, rows, cols], kk_ref[st, win, cols],
                                         (((1,), (1,)), ((), ())), preferred_element_type=F32)
                logits = logits + bias_ref[h]
                if j == 0:
                    logits = jnp.where(first_tile & (key_col < BAND), NEG_INF, logits)
                mx = jnp.max(logits, axis=-1, keepdims=True)
                p = jnp.exp2(logits - mx)
                pv = _dot(p.astype(BF16),
                          vv_ref[st, win, 2 * h * HEAD_DIM:(2 * h + 2) * HEAD_DIM])
                s = pv[:, HEAD_DIM:]
                o_ref[0, st, rows, cols] = (pv[:, :HEAD_DIM] / s).astype(o_ref.dtype)
                lse_tile = jnp.where(lane == h, mx + jnp.log2(s), lse_tile)
            lse_ref[0, st, rows, :] = lse_tile


def _group_attention(qkv, bias, group):
    batch, d, u, _ = qkv.shape
    tq = min(ATTN_TILE, u)
    ns = min(d, ATTN_TILE // tq)
    nblk = tq // BAND

    def slab(which):
        return lambda b, r, i: (b, r, i, which)

    def prev_slab(which):
        return lambda b, r, i: (b, r, jnp.maximum(i * nblk - 1, 0), which)

    return pl.pallas_call(
        _attn_kernel,
        grid=(batch, d // ns, u // tq),
        in_specs=[pl.BlockSpec((1, ns, tq, GROUP_WIDTH), slab(0)),
                  pl.BlockSpec((1, ns, tq, GROUP_WIDTH), slab(1)),
                  pl.BlockSpec((1, ns, tq, GROUP_WIDTH), slab(2)),
                  pl.BlockSpec((1, ns, BAND, GROUP_WIDTH), prev_slab(1)),
                  pl.BlockSpec((1, ns, BAND, GROUP_WIDTH), prev_slab(2)),
                  pl.BlockSpec((HEADS_PER_GROUP, BAND, 2 * BAND), lambda b, r, i: (group, 0, 0))],
        out_specs=[pl.BlockSpec((1, ns, tq, GROUP_WIDTH), lambda b, r, i: (b, r, i, 0)),
                   pl.BlockSpec((1, ns, tq, LANES), lambda b, r, i: (b, r, i, 0))],
        out_shape=[jax.ShapeDtypeStruct((batch, d, u, GROUP_WIDTH), BF16),
                   jax.ShapeDtypeStruct((batch, d, u, LANES), F32)],
        scratch_shapes=[pltpu.VMEM((ns, BAND + tq, GROUP_WIDTH), BF16),
                        pltpu.VMEM((ns, BAND + tq, 2 * GROUP_WIDTH), BF16)],
        compiler_params=_params("parallel", "parallel", "parallel"),
        name=f"attn_group{group}",
    )(qkv, qkv, qkv, qkv, qkv, bias)


def _stream_to_token_permutation(tm, d):
    n = tm // d
    t = np.arange(tm)
    perm = np.zeros((tm, tm), np.float32)
    perm[t, (t % d) * n + t // d] = 1.0
    return perm


def _combine_tile(x_ref, o0_ref, o1_ref, o2_ref, l0_ref, l1_ref, l2_ref, p2_ref,
                  w_ref, g_ref, lt_ref, ot_ref):
    tm = x_ref.shape[0]
    for gi, l_ref in enumerate((l1_ref, l2_ref)):
        d = DILATIONS[gi + 1]
        n = tm // d
        for r in range(d):
            lt_ref[gi, pl.ds(r, n, stride=d), :] = l_ref[0, r]
    d1 = DILATIONS[1]
    for r in range(d1):
        for j in range(N_SLABS):
            ot_ref[j, pl.ds(r, tm // d1, stride=d1), :] = (
                o1_ref[0, r, :, j * LANES:(j + 1) * LANES].astype(F32))
    o2_tok = _dot(p2_ref[...], o2_ref[0].reshape(tm, GROUP_WIDTH))
    o_tok = [lambda h: ot_ref[h],
             lambda h: o2_tok[:, h * HEAD_DIM:(h + 1) * HEAD_DIM]]
    lses = [l0_ref[0, 0], lt_ref[0], lt_ref[1]]
    mx = jnp.maximum(jnp.maximum(lses[0], lses[1]), lses[2])
    es = [jnp.exp2(l - mx) for l in lses]
    inv = 1.0 / (es[0] + es[1] + es[2])
    alphas = [e * inv for e in es[1:]]
    parts = []
    for h in range(HEADS_PER_GROUP):
        cols = slice(h * HEAD_DIM, (h + 1) * HEAD_DIM)
        base = o0_ref[0, 0, :, cols].astype(F32)
        acc = base
        for gi in range(N_GROUPS - 1):
            acc = acc + alphas[gi][:, h:h + 1] * (o_tok[gi](h) - base)
        parts.append(acc.astype(BF16))
    o = jnp.concatenate(parts, axis=1)
    return x_ref[...] + _rms(_dot(o, w_ref[...]), g_ref[...])


def _combine_kernel(x_ref, o0_ref, o1_ref, o2_ref, l0_ref, l1_ref, l2_ref, p2_ref,
                    w_ref, g_ref, out_ref, lt_ref, ot_ref):
    out_ref[0] = _combine_tile(x_ref.at[0], o0_ref, o1_ref, o2_ref, l0_ref, l1_ref, l2_ref,
                               p2_ref, w_ref, g_ref, lt_ref, ot_ref)


def _combine_project(x, outs, lses, w_o, g_post):
    batch, seq, d_model = x.shape
    tm = min(ROW_TILE, seq)
    tile = pl.BlockSpec((1, tm, d_model), lambda b, i: (b, i, 0))
    stream = lambda d, width: pl.BlockSpec((1, d, tm // d, width), lambda b, i: (b, 0, i, 0))
    perm = jnp.asarray(_stream_to_token_permutation(tm, DILATIONS[2]), BF16)
    return pl.pallas_call(
        _combine_kernel,
        grid=(batch, seq // tm),
        in_specs=[tile] + [stream(d, GROUP_WIDTH) for d in DILATIONS]
                 + [stream(d, LANES) for d in DILATIONS]
                 + [_resident(perm.shape), _resident(w_o.shape), _resident((1, d_model))],
        out_specs=tile,
        out_shape=jax.ShapeDtypeStruct(x.shape, F32),
        scratch_shapes=[pltpu.VMEM((N_GROUPS - 1, tm, LANES), F32),
                        pltpu.VMEM((N_SLABS, tm, LANES), F32)],
        compiler_params=_params("parallel", "parallel"),
        name="attn_combine_out",
    )(x, *outs, *lses, perm, w_o, g_post.reshape(1, d_model))


def _attention_layer(x, g_pre, g_post, w_qkv, w_o, bias):
    qkvs = _qkv_proj(x, g_pre, w_qkv.astype(BF16))
    outs, lses = [], []
    for group in range(N_GROUPS):
        o, lse = _group_attention(qkvs[group], bias, group)
        outs.append(o)
        lses.append(lse)
    return _combine_project(x, outs, lses, w_o.astype(BF16), g_post)


SEGMENTS = SUBLANES
SEG_PITCH_PAD = 8


def _to_interleaved(h, stage_ref, dst_ref):
    tm, d = h.shape
    seg = tm // SEGMENTS
    pitch = seg + SEG_PITCH_PAD
    for s in range(SEGMENTS):
        for j in range(d // LANES):
            stage_ref[j, s * pitch:s * pitch + seg, :] = h[s * seg:(s + 1) * seg, j * LANES:(j + 1) * LANES]
    for i in range(0, seg, 2):
        for j in range(d // LANES):
            pair = [stage_ref[j, pl.ds(i + e, SEGMENTS, stride=pitch), :] for e in range(2)]
            dst_ref[i * SEGMENTS:(i + 2) * SEGMENTS, j * LANES:(j + 1) * LANES] = (
                jnp.concatenate(pair, axis=0).astype(dst_ref.dtype))


def _residual_from_interleaved(y, stage_ref, x_ref, out_ref):
    tm, d = y.shape
    seg = tm // SEGMENTS
    pitch = seg + SEG_PITCH_PAD
    for i in range(seg):
        for j in range(d // LANES):
            stage_ref[j, pl.ds(i, SEGMENTS, stride=pitch), :] = (
                y[i * SEGMENTS:(i + 1) * SEGMENTS, j * LANES:(j + 1) * LANES])
    for s in range(SEGMENTS):
        rows = slice(s * seg, (s + 1) * seg)
        for j in range(d // LANES):
            cols = slice(j * LANES, (j + 1) * LANES)
            out_ref[rows, cols] = x_ref[rows, cols] + stage_ref[j, s * pitch:s * pitch + seg, :]


def _stage_conv_input(u, dst_ref, dst_cols, halo_ref, halo_cols, taps):
    tm, width = u.shape
    seg = tm // SEGMENTS
    head = (taps - 1) * SEGMENTS
    row = lax.broadcasted_iota(jnp.int32, (SEGMENTS, width), 0)
    dst_ref[head:head + tm, dst_cols] = u
    for k in range(1, taps):
        grp = u[(seg - k) * SEGMENTS:(seg - k + 1) * SEGMENTS, :]
        before = jnp.where(row == 0, halo_ref[k - 1:k, halo_cols], pltpu.roll(grp, 1, 0))
        dst_ref[head - k * SEGMENTS:head - (k - 1) * SEGMENTS, dst_cols] = before
        halo_ref[k - 1:k, halo_cols] = grp[SEGMENTS - 1:SEGMENTS, :]


def _causal_conv(src_ref, src_cols, tm, w_ref, w_cols, bias):
    taps = w_ref.shape[0]
    out = bias
    for t in range(taps):
        start = t * SEGMENTS
        out = out + src_ref[start:start + tm, src_cols] * w_ref[t:t + 1, w_cols]
    return out


def _log_sigmoid(x):
    return jnp.minimum(x, 0.0) - jnp.log1p(jnp.exp(-jnp.abs(x)))


def _sublane_scan(q, e):
    row = lax.broadcasted_iota(jnp.int32, q.shape, 0)
    for s in (1, 2, 4):
        q_prev = jnp.where(row >= s, pltpu.roll(q, s, 0), 1.0)
        e_prev = jnp.where(row >= s, pltpu.roll(e, s, 0), 0.0)
        e = q * e_prev + e
        q = q * q_prev
    return e


def _lru_kernel(x_ref, gpre_ref, gpost_ref, win_ref, cw_ref, cb_ref, gaw_ref, gab_ref,
                gxw_ref, gxb_ref, lam_ref, wout_ref, out_ref,
                halo_ref, stage_ref, hp_ref, xs_ref, a_ref, b_ref, p_ref, h_ref):
    n_sub, tm, width = a_ref.shape
    seg = tm // SEGMENTS
    taps = cw_ref.shape[0]

    @pl.when(pl.program_id(1) == 0)
    def _():
        halo_ref[...] = jnp.zeros_like(halo_ref)
        h_ref[...] = jnp.zeros_like(h_ref)

    blk = width // LRU_BLOCKS
    row = lax.broadcasted_iota(jnp.int32, (SEGMENTS, blk), 0)
    log2_decay = (LRU_C * LOG2E) * _log_sigmoid(lam_ref[...])
    block_cols = [slice(n * blk, (n + 1) * blk) for n in range(LRU_BLOCKS)]
    items = [(t, n) for t in range(n_sub) for n in range(LRU_BLOCKS)]
    vals = {item: dict() for item in items}
    hn = {}
    y_parts = {t: [] for t in range(n_sub)}

    def head(t):
        rows = slice(t * tm, (t + 1) * tm)
        _to_interleaved(_rms(x_ref[0, rows, :], gpre_ref[...]), stage_ref.at[t], hp_ref.at[t])
        hn[t] = hp_ref[t]

    def tail(t):
        rows = slice(t * tm, (t + 1) * tm)
        y = functools.reduce(lambda acc, part: acc + part, y_parts[t])
        _residual_from_interleaved(_rms(y, gpost_ref[...]), stage_ref.at[t],
                                   x_ref.at[0, rows], out_ref.at[0, rows])

    def project(item):
        t, n = item
        vals[item]["u"] = _dot(hn[t], win_ref[:, block_cols[n]])
        vals[item]["g"] = _dot(hn[t], win_ref[:, width + n * blk:width + (n + 1) * blk])

    def conv(item):
        t, n = item
        cols = block_cols[n]
        _stage_conv_input(vals[item].pop("u"), xs_ref.at[t], cols, halo_ref, cols, taps)
        vals[item]["xc"] = _causal_conv(xs_ref.at[t], cols, tm, cw_ref, cols, cb_ref[:, cols])
        vals[item]["gate"] = _gelu_tanh(vals[item].pop("g"))

    def gates(item):
        n = item[1]
        xb = vals[item]["xc"].astype(BF16)
        vals[item]["r"] = _dot(xb, gaw_ref[n])
        vals[item]["i"] = _dot(xb, gxw_ref[n])

    def decay(item):
        t, n = item
        cols = block_cols[n]
        r = _sigmoid(vals[item].pop("r") + gab_ref[n:n + 1, :])
        gi = _sigmoid(vals[item].pop("i") + gxb_ref[n:n + 1, :])
        a = jnp.exp2(r * log2_decay[:, cols])
        a_ref[t, :, cols] = a
        b_ref[t, :, cols] = (jnp.exp2(0.5 * jnp.log2(jnp.maximum(1.0 - a * a, 0.0)))
                             * gi * vals[item].pop("xc"))

    def scan(item):
        t, n = item
        cols = block_cols[n]
        h = jnp.zeros((SEGMENTS, blk), F32)
        p = jnp.ones((SEGMENTS, blk), F32)
        for i in range(seg):
            rows = slice(i * SEGMENTS, (i + 1) * SEGMENTS)
            a_i = a_ref[t, rows, cols]
            h = a_i * h + b_ref[t, rows, cols]
            p = a_i * p
            b_ref[t, rows, cols] = h
            p_ref[t, rows, cols] = p
        e = jnp.where(row == 0, h_ref[:, cols], pltpu.roll(h, 1, 0))
        q = jnp.where(row == 0, 0.0, pltpu.roll(p, 1, 0))
        entry = _sublane_scan(q, e)
        h_ref[:, cols] = (h + p * entry)[SEGMENTS - 1:SEGMENTS, :]
        vals[item]["entry"] = entry

    def project_out(item):
        t, n = item
        cols = block_cols[n]
        hs = (b_ref[t, :, cols].reshape(seg, SEGMENTS, blk)
              + p_ref[t, :, cols].reshape(seg, SEGMENTS, blk) * vals[item].pop("entry")[None]
              ).reshape(tm, blk)
        y_parts[t].append(_dot((hs * vals[item].pop("gate")).astype(BF16), wout_ref[cols, :]))

    stages = (project, conv, gates, decay, scan, project_out)
    head(0)
    for slot in range(len(items) + len(stages) - 1):
        for k, stage in enumerate(stages):
            if 0 <= slot - k < len(items):
                item = items[slot - k]
                stage(item)
                if stage is project_out and item[1] == LRU_BLOCKS - 1:
                    tail(item[0])
        if slot == 0:
            for t in range(1, n_sub):
                head(t)


def _interleave_scratch(tm, d):
    seg = tm // SEGMENTS
    return [pltpu.VMEM((d // LANES, SEGMENTS * (seg + SEG_PITCH_PAD), LANES), F32),
            pltpu.VMEM((tm, d), BF16)]


def _lru_layer(x, g_pre, g_post, w_in, conv_w, conv_b, ga_w, ga_b, gx_w, gx_b, lam, w_out):
    batch, seq, d = x.shape
    width = w_out.shape[0]
    taps = conv_w.shape[0]
    tm = min(LRU_SUB_TILE, seq)
    n_sub = max(1, min(LRU_STEP_TOKENS, seq) // tm)
    seg = tm // SEGMENTS
    tile = pl.BlockSpec((1, n_sub * tm, d), lambda b, i: (b, i, 0))
    ops = [g_pre.reshape(1, d), g_post.reshape(1, d), w_in.astype(BF16), conv_w,
           conv_b.reshape(1, width), ga_w.astype(BF16), ga_b, gx_w.astype(BF16), gx_b,
           lam.reshape(1, width), w_out.astype(BF16)]
    return pl.pallas_call(
        _lru_kernel,
        grid=(batch, seq // (n_sub * tm)),
        in_specs=[tile] + [_resident(o.shape) for o in ops],
        out_specs=tile,
        out_shape=jax.ShapeDtypeStruct(x.shape, F32),
        scratch_shapes=[pltpu.VMEM((taps - 1, width), F32),
                        pltpu.VMEM((n_sub, d // LANES, SEGMENTS * (seg + SEG_PITCH_PAD), LANES), F32),
                        pltpu.VMEM((n_sub, tm, d), BF16),
                        pltpu.VMEM((n_sub, (taps - 1) * SEGMENTS + tm, width), F32),
                        pltpu.VMEM((n_sub, tm, width), F32),
                        pltpu.VMEM((n_sub, tm, width), F32),
                        pltpu.VMEM((n_sub, tm, width), F32),
                        pltpu.VMEM((1, width), F32)],
        compiler_params=_params("parallel", "arbitrary"),
        name="rglru_layer",
    )(x, *ops)


FFN_CHUNK = 1024


def _ffn_tile(x_ref, out_ref, gpre_ref, gpost_ref, wup_ref, cw_ref, cb_ref, wdn_ref,
              halo_ref, stage_ref, hp_ref, us_g, us_v, acc_ref):
    tm = acc_ref.shape[0]
    d_ff = wdn_ref.shape[0]
    taps = cw_ref.shape[0]
    _to_interleaved(_rms(x_ref[...], gpre_ref[...]), stage_ref, hp_ref)
    hn = hp_ref[...]
    n_chunks = d_ff // FFN_CHUNK
    y = None
    for c in range(n_chunks):
        halves = []
        for part, off in enumerate((c * FFN_CHUNK, d_ff + c * FFN_CHUNK)):
            src = slice(off, off + FFN_CHUNK)
            us_ref = (us_g, us_v)[part]
            _stage_conv_input(_dot(hn, wup_ref[:, src]), us_ref, slice(None), halo_ref, src, taps)
            halves.append(_causal_conv(us_ref, slice(None), tm, cw_ref, src, cb_ref[:, src]))
        act = (_gelu_tanh(halves[0]) * halves[1]).astype(BF16)
        contrib = _dot(act, wdn_ref[c * FFN_CHUNK:(c + 1) * FFN_CHUNK, :])
        if c == n_chunks - 1:
            y = contrib if c == 0 else acc_ref[...] + contrib
        elif c == 0:
            acc_ref[...] = contrib
        else:
            acc_ref[...] += contrib
    _residual_from_interleaved(_rms(y, gpost_ref[...]), stage_ref, x_ref, out_ref)


def _ffn_scratch(tm, d, d_ff, taps):
    return ([pltpu.VMEM((taps - 1, 2 * d_ff), F32)] + _interleave_scratch(tm, d)
            + [pltpu.VMEM(((taps - 1) * SEGMENTS + tm, FFN_CHUNK), F32)] * 2
            + [pltpu.VMEM((tm, d), F32)])


def _ffn_operands(g_pre, g_post, w_up, conv_w, conv_b, w_down):
    d, d_ff = w_down.shape[1], w_down.shape[0]
    return [g_pre.reshape(1, d), g_post.reshape(1, d), w_up.astype(BF16), conv_w,
            conv_b.reshape(1, 2 * d_ff), w_down.astype(BF16)]


def _ffn_kernel(x_ref, gpre_ref, gpost_ref, wup_ref, cw_ref, cb_ref, wdn_ref, out_ref,
                halo_ref, *scratch):
    @pl.when(pl.program_id(1) == 0)
    def _():
        halo_ref[...] = jnp.zeros_like(halo_ref)

    _ffn_tile(x_ref.at[0], out_ref.at[0], gpre_ref, gpost_ref, wup_ref, cw_ref, cb_ref, wdn_ref,
              halo_ref, *scratch)


def _ffn_layer(x, g_pre, g_post, w_up, conv_w, conv_b, w_down):
    batch, seq, d = x.shape
    tm = min(ROW_TILE, seq)
    tile = pl.BlockSpec((1, tm, d), lambda b, i: (b, i, 0))
    ops = _ffn_operands(g_pre, g_post, w_up, conv_w, conv_b, w_down)
    return pl.pallas_call(
        _ffn_kernel,
        grid=(batch, seq // tm),
        in_specs=[tile] + [_resident(o.shape) for o in ops],
        out_specs=tile,
        out_shape=jax.ShapeDtypeStruct(x.shape, F32),
        scratch_shapes=_ffn_scratch(tm, d, w_down.shape[0], conv_w.shape[0]),
        compiler_params=_params("parallel", "arbitrary"),
        name="conv_ffn",
    )(x, *ops)


def kernel(x, norm_mix_pre, norm_mix_post, norm_ffn_pre, norm_ffn_post, rel_bias, attn_w_qkv, attn_w_o, lru_w_in, lru_conv_w, lru_conv_b, lru_ga_w, lru_ga_b, lru_gx_w, lru_gx_b, lru_lambda, lru_w_out, ffn_w_up, ffn_conv_w, ffn_conv_b, ffn_w_down):
    bias = _bias_tables(rel_bias)
    for layer in range(norm_mix_pre.shape[0]):
        j = layer // 2
        ffn = (norm_ffn_pre[layer], norm_ffn_post[layer], ffn_w_up[layer],
               ffn_conv_w[layer], ffn_conv_b[layer], ffn_w_down[layer])
        if layer % 2 == 0:
            x = _attention_layer(x, norm_mix_pre[layer], norm_mix_post[layer],
                                 attn_w_qkv[j], attn_w_o[j], bias)
        else:
            x = _lru_layer(x, norm_mix_pre[layer], norm_mix_post[layer], lru_w_in[j],
                           lru_conv_w[j], lru_conv_b[j], lru_ga_w[j], lru_ga_b[j],
                           lru_gx_w[j], lru_gx_b[j], lru_lambda[j], lru_w_out[j])
        x = _ffn_layer(x, *ffn)
    return x
```

```python
import functools

import numpy as np
import jax
import jax.numpy as jnp
from jax import lax
from jax.experimental import pallas as pl
from jax.experimental.pallas import tpu as pltpu

F32 = jnp.float32
BF16 = jnp.bfloat16

RMS_EPS = 1e-6
NEG_INF = -1e30
HEAD_DIM = 128
HEADS_PER_GROUP = 8
WINDOWS = (128, 512, 2048)
DILATIONS = (1, 4, 16)
N_GROUPS = 3
BAND = 128
NUM_BUCKETS = 32
MAX_DISTANCE = 2048
LRU_C = 8.0
LOG2E = float(np.log2(np.e))
Q_SCALE = HEAD_DIM ** -0.5 * LOG2E
LRU_BLOCKS = 4
SUBLANES = 8
LANES = 128
VMEM_LIMIT_BYTES = 56 * 1024 * 1024
ROW_TILE = 512
ATTN_TILE = 1024


def _params(*semantics):
    return pltpu.CompilerParams(dimension_semantics=semantics,
                                vmem_limit_bytes=VMEM_LIMIT_BYTES)


def _resident(shape):
    nd = len(shape)
    return pl.BlockSpec(shape, lambda *_: (0,) * nd, pipeline_mode=pl.Buffered(1))


def _rms(x, g):
    return x * lax.rsqrt(jnp.mean(x * x, axis=-1, keepdims=True) + RMS_EPS) * g


def _dot(a, b):
    return jnp.dot(a, b, preferred_element_type=F32)


def _gelu_tanh(x):
    c = float(np.sqrt(2.0 / np.pi))
    t = jnp.tanh(x * (c + (c * 0.044715) * (x * x)))
    return x * (0.5 + 0.5 * t)


def _sigmoid(x):
    return 0.5 + 0.5 * jnp.tanh(0.5 * x)


def _t5_bucket(dist):
    max_exact = NUM_BUCKETS // 2
    d = np.maximum(dist, 1).astype(np.float64)
    large = max_exact + (np.log(d / max_exact) / np.log(MAX_DISTANCE / max_exact)
                         * (NUM_BUCKETS - max_exact)).astype(np.int32)
    large = np.minimum(large, NUM_BUCKETS - 1)
    return np.where(dist < max_exact, dist, large).astype(np.int32)


def _bucket_tables():
    i = np.arange(BAND)[:, None]
    k = np.arange(2 * BAND)[None, :]
    m = i + BAND - k
    valid = (m >= 0) & (m <= BAND)
    tabs = [np.where(valid, _t5_bucket(np.clip(m, 0, BAND) * d), -1) for d in DILATIONS]
    return np.stack(tabs).astype(np.int32)


def _bias_kernel(tbl_ref, bucket_ref, out_ref):
    h = pl.program_id(0)
    bucket = bucket_ref[0]
    acc = jnp.where(bucket < 0, NEG_INF, 0.0).astype(F32)
    for j in range(NUM_BUCKETS):
        acc = jnp.where(bucket == j, tbl_ref[j, h], acc)
    out_ref[0] = acc * LOG2E


def _bias_tables(rel_bias):
    n_heads = rel_bias.shape[1]
    buckets = jnp.asarray(_bucket_tables())
    return pl.pallas_call(
        _bias_kernel,
        grid=(n_heads,),
        in_specs=[pl.BlockSpec(memory_space=pltpu.SMEM),
                  pl.BlockSpec((1, BAND, 2 * BAND), lambda h: (h // HEADS_PER_GROUP, 0, 0))],
        out_specs=pl.BlockSpec((1, BAND, 2 * BAND), lambda h: (h, 0, 0)),
        out_shape=jax.ShapeDtypeStruct((n_heads, BAND, 2 * BAND), F32),
        compiler_params=_params("arbitrary"),
        name="bias_tables",
    )(rel_bias.astype(F32), buckets)


GROUP_WIDTH = HEADS_PER_GROUP * HEAD_DIM
N_SLABS = GROUP_WIDTH // LANES


def _qkv_kernel(x_ref, g_ref, w_ref, o0_ref, o1_ref, o2_ref, hs_ref, hp_ref):
    tm = x_ref.shape[1]
    h = _rms(x_ref[0], g_ref[...])
    hb = h.astype(BF16)
    for c in range(3):
        cols = slice(c * GROUP_WIDTH, (c + 1) * GROUP_WIDTH)
        res = _dot(hb, w_ref[:, cols])
        o0_ref[0, 0, :, cols] = (res * Q_SCALE if c == 0 else res).astype(o0_ref.dtype)
    for j in range(N_SLABS):
        hs_ref[j] = h[:, j * LANES:(j + 1) * LANES]
    for group, o_ref in ((1, o1_ref), (2, o2_ref)):
        d = DILATIONS[group]
        n = tm // d
        for r in range(d):
            for j in range(N_SLABS):
                hp_ref[r * n:(r + 1) * n, j * LANES:(j + 1) * LANES] = (
                    hs_ref[j, pl.ds(r, n, stride=d), :].astype(BF16))
        hp = hp_ref[...]
        for c in range(3):
            cols = slice(c * GROUP_WIDTH, (c + 1) * GROUP_WIDTH)
            wcols = slice((3 * group + c) * GROUP_WIDTH, (3 * group + c + 1) * GROUP_WIDTH)
            res = _dot(hp, w_ref[:, wcols])
            res = (res * Q_SCALE if c == 0 else res).astype(o_ref.dtype)
            for r in range(d):
                o_ref[0, r, :, cols] = res[r * n:(r + 1) * n, :]


def _qkv_proj(x, g, w):
    batch, seq, d_model = x.shape
    tm = min(ROW_TILE, seq)
    out_specs, out_shape = [], []
    for d in DILATIONS:
        out_specs.append(pl.BlockSpec((1, d, tm // d, 3 * GROUP_WIDTH), lambda b, i: (b, 0, i, 0)))
        out_shape.append(jax.ShapeDtypeStruct((batch, d, seq // d, 3 * GROUP_WIDTH), BF16))
    return pl.pallas_call(
        _qkv_kernel,
        grid=(batch, seq // tm),
        in_specs=[pl.BlockSpec((1, tm, d_model), lambda b, i: (b, i, 0)),
                  _resident((1, d_model)), _resident(w.shape)],
        out_specs=out_specs,
        out_shape=out_shape,
        scratch_shapes=[pltpu.VMEM((N_SLABS, tm, LANES), F32),
                        pltpu.VMEM((tm, d_model), BF16)],
        compiler_params=_params("parallel", "parallel"),
        name="qkv_proj",
    )(x, g.reshape(1, d_model), w)


def _attn_kernel(qkv_ref, prev_ref, bias_ref, o_ref, lse_ref, kk_ref, vv_ref):
    n_streams, tq = qkv_ref.shape[1], qkv_ref.shape[2]
    k0, v0 = GROUP_WIDTH, 2 * GROUP_WIDTH
    first_tile = pl.program_id(2) == 0
    ones = jnp.ones((BAND + tq, HEAD_DIM), BF16)
    lane = lax.broadcasted_iota(jnp.int32, (BAND, LANES), 1)
    key_col = lax.broadcasted_iota(jnp.int32, (BAND, 2 * BAND), 1)
    for st in range(n_streams):
        kk_ref[st, 0:BAND, :] = prev_ref[0, st, :, k0:v0]
        kk_ref[st, BAND:, :] = qkv_ref[0, st, :, k0:v0]
        for h in range(HEADS_PER_GROUP):
            cols = slice(v0 + h * HEAD_DIM, v0 + (h + 1) * HEAD_DIM)
            vv_ref[st, 0:BAND, 2 * h * HEAD_DIM:(2 * h + 1) * HEAD_DIM] = prev_ref[0, st, :, cols]
            vv_ref[st, BAND:, 2 * h * HEAD_DIM:(2 * h + 1) * HEAD_DIM] = qkv_ref[0, st, :, cols]
            vv_ref[st, :, (2 * h + 1) * HEAD_DIM:(2 * h + 2) * HEAD_DIM] = ones
        for j in range(tq // BAND):
            rows = slice(j * BAND, (j + 1) * BAND)
            win = slice(j * BAND, (j + 2) * BAND)
            lse_tile = jnp.zeros((BAND, LANES), F32)
            for h in range(HEADS_PER_GROUP):
                cols = slice(h * HEAD_DIM, (h + 1) * HEAD_DIM)
                logits = lax.dot_general(qkv_ref[0, st, rows, cols], kk_ref[st, win, cols],
                                         (((1,), (1,)), ((), ())), preferred_element_type=F32)
                logits = logits + bias_ref[h]
                if j == 0:
                    logits = jnp.where(first_tile & (key_col < BAND), NEG_INF, logits)
                mx = jnp.max(logits, axis=-1, keepdims=True)
                p = jnp.exp2(logits - mx)
                pv = _dot(p.astype(BF16),
                          vv_ref[st, win, 2 * h * HEAD_DIM:(2 * h + 2) * HEAD_DIM])
                s = pv[:, HEAD_DIM:]
                o_ref[0, st, rows, cols] = (pv[:, :HEAD_DIM] / s).astype(o_ref.dtype)
                lse_tile = jnp.where(lane == h, mx + jnp.log2(s), lse_tile)
            lse_ref[0, st, rows, :] = lse_tile


def _group_attention(qkv, bias, group):
    batch, d, u, _ = qkv.shape
    tq = min(ATTN_TILE, u)
    ns = min(d, ATTN_TILE // tq)
    nblk = tq // BAND
    width = qkv.shape[3]
    return pl.pallas_call(
        _attn_kernel,
        grid=(batch, d // ns, u // tq),
        in_specs=[pl.BlockSpec((1, ns, tq, width), lambda b, r, i: (b, r, i, 0)),
                  pl.BlockSpec((1, ns, BAND, width),
                               lambda b, r, i: (b, r, jnp.maximum(i * nblk - 1, 0), 0)),
                  pl.BlockSpec((HEADS_PER_GROUP, BAND, 2 * BAND), lambda b, r, i: (group, 0, 0))],
        out_specs=[pl.BlockSpec((1, ns, tq, GROUP_WIDTH), lambda b, r, i: (b, r, i, 0)),
                   pl.BlockSpec((1, ns, tq, LANES), lambda b, r, i: (b, r, i, 0))],
        out_shape=[jax.ShapeDtypeStruct((batch, d, u, GROUP_WIDTH), BF16),
                   jax.ShapeDtypeStruct((batch, d, u, LANES), F32)],
        scratch_shapes=[pltpu.VMEM((ns, BAND + tq, GROUP_WIDTH), BF16),
                        pltpu.VMEM((ns, BAND + tq, 2 * GROUP_WIDTH), BF16)],
        compiler_params=_params("parallel", "parallel", "parallel"),
        name=f"attn_group{group}",
    )(qkv, qkv, bias)


def _stream_to_token_permutation(tm, d):
    n = tm // d
    t = np.arange(tm)
    perm = np.zeros((tm, tm), np.float32)
    perm[t, (t % d) * n + t // d] = 1.0
    return perm


def _combine_tile(x_ref, o0_ref, o1_ref, o2_ref, l0_ref, l1_ref, l2_ref, p1_ref, p2_ref,
                  w_ref, g_ref, lt_ref):
    tm = x_ref.shape[0]
    o_tok = []
    for gi, (o_ref, l_ref, p_ref) in enumerate(((o1_ref, l1_ref, p1_ref), (o2_ref, l2_ref, p2_ref))):
        d = DILATIONS[gi + 1]
        n = tm // d
        for r in range(d):
            lt_ref[gi, pl.ds(r, n, stride=d), :] = l_ref[0, r]
        o_tok.append(_dot(p_ref[...], o_ref[0].reshape(tm, GROUP_WIDTH)))
    lses = [l0_ref[0, 0], lt_ref[0], lt_ref[1]]
    mx = jnp.maximum(jnp.maximum(lses[0], lses[1]), lses[2])
    es = [jnp.exp2(l - mx) for l in lses]
    inv = 1.0 / (es[0] + es[1] + es[2])
    alphas = [e * inv for e in es[1:]]
    parts = []
    for h in range(HEADS_PER_GROUP):
        cols = slice(h * HEAD_DIM, (h + 1) * HEAD_DIM)
        base = o0_ref[0, 0, :, cols].astype(F32)
        acc = base
        for gi in range(N_GROUPS - 1):
            acc = acc + alphas[gi][:, h:h + 1] * (o_tok[gi][:, cols] - base)
        parts.append(acc.astype(BF16))
    o = jnp.concatenate(parts, axis=1)
    return x_ref[...] + _rms(_dot(o, w_ref[...]), g_ref[...])


def _combine_kernel(x_ref, o0_ref, o1_ref, o2_ref, l0_ref, l1_ref, l2_ref, p1_ref, p2_ref,
                    w_ref, g_ref, out_ref, lt_ref):
    out_ref[0] = _combine_tile(x_ref.at[0], o0_ref, o1_ref, o2_ref, l0_ref, l1_ref, l2_ref,
                               p1_ref, p2_ref, w_ref, g_ref, lt_ref)


def _combine_project(x, outs, lses, w_o, g_post):
    batch, seq, d_model = x.shape
    tm = min(ROW_TILE, seq)
    tile = pl.BlockSpec((1, tm, d_model), lambda b, i: (b, i, 0))
    stream = lambda d, width: pl.BlockSpec((1, d, tm // d, width), lambda b, i: (b, 0, i, 0))
    perms = [jnp.asarray(_stream_to_token_permutation(tm, d), BF16) for d in DILATIONS[1:]]
    return pl.pallas_call(
        _combine_kernel,
        grid=(batch, seq // tm),
        in_specs=[tile] + [stream(d, GROUP_WIDTH) for d in DILATIONS]
                 + [stream(d, LANES) for d in DILATIONS]
                 + [_resident(p.shape) for p in perms]
                 + [_resident(w_o.shape), _resident((1, d_model))],
        out_specs=tile,
        out_shape=jax.ShapeDtypeStruct(x.shape, F32),
        scratch_shapes=[pltpu.VMEM((N_GROUPS - 1, tm, LANES), F32)],
        compiler_params=_params("parallel", "parallel"),
        name="attn_combine_out",
    )(x, *outs, *lses, *perms, w_o, g_post.reshape(1, d_model))


def _attention_layer(x, g_pre, g_post, w_qkv, w_o, bias):
    qkvs = _qkv_proj(x, g_pre, w_qkv.astype(BF16))
    outs, lses = [], []
    for group in range(N_GROUPS):
        o, lse = _group_attention(qkvs[group], bias, group)
        outs.append(o)
        lses.append(lse)
    return _combine_project(x, outs, lses, w_o.astype(BF16), g_post)


SEGMENTS = SUBLANES
SEG_PITCH_PAD = 8


def _to_interleaved(h, stage_ref, dst_ref):
    tm, d = h.shape
    seg = tm // SEGMENTS
    pitch = seg + SEG_PITCH_PAD
    for s in range(SEGMENTS):
        for j in range(d // LANES):
            stage_ref[j, s * pitch:s * pitch + seg, :] = h[s * seg:(s + 1) * seg, j * LANES:(j + 1) * LANES]
    for i in range(0, seg, 2):
        for j in range(d // LANES):
            pair = [stage_ref[j, pl.ds(i + e, SEGMENTS, stride=pitch), :] for e in range(2)]
            dst_ref[i * SEGMENTS:(i + 2) * SEGMENTS, j * LANES:(j + 1) * LANES] = (
                jnp.concatenate(pair, axis=0).astype(dst_ref.dtype))


def _residual_from_interleaved(y, stage_ref, x_ref, out_ref):
    tm, d = y.shape
    seg = tm // SEGMENTS
    pitch = seg + SEG_PITCH_PAD
    for i in range(seg):
        for j in range(d // LANES):
            stage_ref[j, pl.ds(i, SEGMENTS, stride=pitch), :] = (
                y[i * SEGMENTS:(i + 1) * SEGMENTS, j * LANES:(j + 1) * LANES])
    for s in range(SEGMENTS):
        rows = slice(s * seg, (s + 1) * seg)
        for j in range(d // LANES):
            cols = slice(j * LANES, (j + 1) * LANES)
            out_ref[rows, cols] = x_ref[rows, cols] + stage_ref[j, s * pitch:s * pitch + seg, :]


def _stage_conv_input(u, dst_ref, dst_cols, halo_ref, halo_cols, taps):
    tm, width = u.shape
    seg = tm // SEGMENTS
    head = (taps - 1) * SEGMENTS
    row = lax.broadcasted_iota(jnp.int32, (SEGMENTS, width), 0)
    dst_ref[head:head + tm, dst_cols] = u
    for k in range(1, taps):
        grp = u[(seg - k) * SEGMENTS:(seg - k + 1) * SEGMENTS, :]
        before = jnp.where(row == 0, halo_ref[k - 1:k, halo_cols], pltpu.roll(grp, 1, 0))
        dst_ref[head - k * SEGMENTS:head - (k - 1) * SEGMENTS, dst_cols] = before
        halo_ref[k - 1:k, halo_cols] = grp[SEGMENTS - 1:SEGMENTS, :]


def _causal_conv(src_ref, src_cols, tm, w_ref, w_cols, bias):
    taps = w_ref.shape[0]
    out = bias
    for t in range(taps):
        start = t * SEGMENTS
        out = out + src_ref[start:start + tm, src_cols] * w_ref[t:t + 1, w_cols]
    return out


def _log_sigmoid(x):
    return jnp.minimum(x, 0.0) - jnp.log1p(jnp.exp(-jnp.abs(x)))


def _sublane_scan(q, e):
    row = lax.broadcasted_iota(jnp.int32, q.shape, 0)
    for s in (1, 2, 4):
        q_prev = jnp.where(row >= s, pltpu.roll(q, s, 0), 1.0)
        e_prev = jnp.where(row >= s, pltpu.roll(e, s, 0), 0.0)
        e = q * e_prev + e
        q = q * q_prev
    return e


def _lru_kernel(x_ref, gpre_ref, gpost_ref, win_ref, cw_ref, cb_ref, gaw_ref, gab_ref,
                gxw_ref, gxb_ref, lam_ref, wout_ref, out_ref,
                halo_ref, stage_ref, hp_ref, xs_ref, a_ref, b_ref, p_ref, h_ref):
    n_sub, tm, width = a_ref.shape
    seg = tm // SEGMENTS
    taps = cw_ref.shape[0]

    @pl.when(pl.program_id(1) == 0)
    def _():
        halo_ref[...] = jnp.zeros_like(halo_ref)
        h_ref[...] = jnp.zeros_like(h_ref)

    blk = width // LRU_BLOCKS
    row = lax.broadcasted_iota(jnp.int32, (SEGMENTS, blk), 0)
    log2_decay = (LRU_C * LOG2E) * _log_sigmoid(lam_ref[...])
    block_cols = [slice(n * blk, (n + 1) * blk) for n in range(LRU_BLOCKS)]
    items = [(t, n) for t in range(n_sub) for n in range(LRU_BLOCKS)]
    vals = {item: dict() for item in items}
    hn = {}
    y_parts = {t: [] for t in range(n_sub)}

    def head(t):
        rows = slice(t * tm, (t + 1) * tm)
        _to_interleaved(_rms(x_ref[0, rows, :], gpre_ref[...]), stage_ref.at[t], hp_ref.at[t])
        hn[t] = hp_ref[t]

    def tail(t):
        rows = slice(t * tm, (t + 1) * tm)
        y = functools.reduce(lambda acc, part: acc + part, y_parts[t])
        _residual_from_interleaved(_rms(y, gpost_ref[...]), stage_ref.at[t],
                                   x_ref.at[0, rows], out_ref.at[0, rows])

    def project(item):
        t, n = item
        vals[item]["u"] = _dot(hn[t], win_ref[:, block_cols[n]])
        vals[item]["g"] = _dot(hn[t], win_ref[:, width + n * blk:width + (n + 1) * blk])

    def conv(item):
        t, n = item
        cols = block_cols[n]
        _stage_conv_input(vals[item].pop("u"), xs_ref.at[t], cols, halo_ref, cols, taps)
        vals[item]["xc"] = _causal_conv(xs_ref.at[t], cols, tm, cw_ref, cols, cb_ref[:, cols])
        vals[item]["gate"] = _gelu_tanh(vals[item].pop("g"))

    def gates(item):
        n = item[1]
        xb = vals[item]["xc"].astype(BF16)
        vals[item]["r"] = _dot(xb, gaw_ref[n])
        vals[item]["i"] = _dot(xb, gxw_ref[n])

    def decay(item):
        t, n = item
        cols = block_cols[n]
        r = _sigmoid(vals[item].pop("r") + gab_ref[n:n + 1, :])
        gi = _sigmoid(vals[item].pop("i") + gxb_ref[n:n + 1, :])
        a = jnp.exp2(r * log2_decay[:, cols])
        a_ref[t, :, cols] = a
        b_ref[t, :, cols] = (jnp.exp2(0.5 * jnp.log2(jnp.maximum(1.0 - a * a, 0.0)))
                             * gi * vals[item].pop("xc"))

    def scan(item):
        t, n = item
        cols = block_cols[n]
        h = jnp.zeros((SEGMENTS, blk), F32)
        p = jnp.ones((SEGMENTS, blk), F32)
        for i in range(seg):
            rows = slice(i * SEGMENTS, (i + 1) * SEGMENTS)
            a_i = a_ref[t, rows, cols]
            h = a_i * h + b_ref[t, rows, cols]
            p = a_i * p
            b_ref[t, rows, cols] = h
            p_ref[t, rows, cols] = p
        e = jnp.where(row == 0, h_ref[:, cols], pltpu.roll(h, 1, 0))
        q = jnp.where(row == 0, 0.0, pltpu.roll(p, 1, 0))
        entry = _sublane_scan(q, e)
        h_ref[:, cols] = (h + p * entry)[SEGMENTS - 1:SEGMENTS, :]
        vals[item]["entry"] = entry

    def project_out(item):
        t, n = item
        cols = block_cols[n]
        hs = (b_ref[t, :, cols].reshape(seg, SEGMENTS, blk)
              + p_ref[t, :, cols].reshape(seg, SEGMENTS, blk) * vals[item].pop("entry")[None]
              ).reshape(tm, blk)
        y_parts[t].append(_dot((hs * vals[item].pop("gate")).astype(BF16), wout_ref[cols, :]))

    stages = (project, conv, gates, decay, scan, project_out)
    head(0)
    for slot in range(len(items) + len(stages) - 1):
        for k, stage in enumerate(stages):
            if 0 <= slot - k < len(items):
                item = items[slot - k]
                stage(item)
                if stage is project_out and item[1] == LRU_BLOCKS - 1:
                    tail(item[0])
        if slot == 0:
            for t in range(1, n_sub):
                head(t)


def _interleave_scratch(tm, d):
    seg = tm // SEGMENTS
    return [pltpu.VMEM((d // LANES, SEGMENTS * (seg + SEG_PITCH_PAD), LANES), F32),
            pltpu.VMEM((tm, d), BF16)]


def _lru_layer(x, g_pre, g_post, w_in, conv_w, conv_b, ga_w, ga_b, gx_w, gx_b, lam, w_out):
    batch, seq, d = x.shape
    width = w_out.shape[0]
    taps = conv_w.shape[0]
    tm = min(ROW_TILE, seq)
    n_sub = 2 if seq % (2 * tm) == 0 else 1
    seg = tm // SEGMENTS
    tile = pl.BlockSpec((1, n_sub * tm, d), lambda b, i: (b, i, 0))
    ops = [g_pre.reshape(1, d), g_post.reshape(1, d), w_in.astype(BF16), conv_w,
           conv_b.reshape(1, width), ga_w.astype(BF16), ga_b, gx_w.astype(BF16), gx_b,
           lam.reshape(1, width), w_out.astype(BF16)]
    return pl.pallas_call(
        _lru_kernel,
        grid=(batch, seq // (n_sub * tm)),
        in_specs=[tile] + [_resident(o.shape) for o in ops],
        out_specs=tile,
        out_shape=jax.ShapeDtypeStruct(x.shape, F32),
        scratch_shapes=[pltpu.VMEM((taps - 1, width), F32),
                        pltpu.VMEM((n_sub, d // LANES, SEGMENTS * (seg + SEG_PITCH_PAD), LANES), F32),
                        pltpu.VMEM((n_sub, tm, d), BF16),
                        pltpu.VMEM((n_sub, (taps - 1) * SEGMENTS + tm, width), F32),
                        pltpu.VMEM((n_sub, tm, width), F32),
                        pltpu.VMEM((n_sub, tm, width), F32),
                        pltpu.VMEM((n_sub, tm, width), F32),
                        pltpu.VMEM((1, width), F32)],
        compiler_params=_params("parallel", "arbitrary"),
        name="rglru_layer",
    )(x, *ops)


FFN_CHUNK = 1024


def _ffn_tile(x_ref, out_ref, gpre_ref, gpost_ref, wup_ref, cw_ref, cb_ref, wdn_ref,
              halo_ref, stage_ref, hp_ref, us_g, us_v, acc_ref):
    tm = acc_ref.shape[0]
    d_ff = wdn_ref.shape[0]
    taps = cw_ref.shape[0]
    _to_interleaved(_rms(x_ref[...], gpre_ref[...]), stage_ref, hp_ref)
    hn = hp_ref[...]
    n_chunks = d_ff // FFN_CHUNK
    y = None
    for c in range(n_chunks):
        halves = []
        for part, off in enumerate((c * FFN_CHUNK, d_ff + c * FFN_CHUNK)):
            src = slice(off, off + FFN_CHUNK)
            us_ref = (us_g, us_v)[part]
            _stage_conv_input(_dot(hn, wup_ref[:, src]), us_ref, slice(None), halo_ref, src, taps)
            halves.append(_causal_conv(us_ref, slice(None), tm, cw_ref, src, cb_ref[:, src]))
        act = (_gelu_tanh(halves[0]) * halves[1]).astype(BF16)
        contrib = _dot(act, wdn_ref[c * FFN_CHUNK:(c + 1) * FFN_CHUNK, :])
        if c == n_chunks - 1:
            y = contrib if c == 0 else acc_ref[...] + contrib
        elif c == 0:
            acc_ref[...] = contrib
        else:
            acc_ref[...] += contrib
    _residual_from_interleaved(_rms(y, gpost_ref[...]), stage_ref, x_ref, out_ref)


def _ffn_scratch(tm, d, d_ff, taps):
    return ([pltpu.VMEM((taps - 1, 2 * d_ff), F32)] + _interleave_scratch(tm, d)
            + [pltpu.VMEM(((taps - 1) * SEGMENTS + tm, FFN_CHUNK), F32)] * 2
            + [pltpu.VMEM((tm, d), F32)])


def _ffn_operands(g_pre, g_post, w_up, conv_w, conv_b, w_down):
    d, d_ff = w_down.shape[1], w_down.shape[0]
    return [g_pre.reshape(1, d), g_post.reshape(1, d), w_up.astype(BF16), conv_w,
            conv_b.reshape(1, 2 * d_ff), w_down.astype(BF16)]


def _ffn_kernel(x_ref, gpre_ref, gpost_ref, wup_ref, cw_ref, cb_ref, wdn_ref, out_ref,
                halo_ref, *scratch):
    @pl.when(pl.program_id(1) == 0)
    def _():
        halo_ref[...] = jnp.zeros_like(halo_ref)

    _ffn_tile(x_ref.at[0], out_ref.at[0], gpre_ref, gpost_ref, wup_ref, cw_ref, cb_ref, wdn_ref,
              halo_ref, *scratch)


def _ffn_layer(x, g_pre, g_post, w_up, conv_w, conv_b, w_down):
    batch, seq, d = x.shape
    tm = min(ROW_TILE, seq)
    tile = pl.BlockSpec((1, tm, d), lambda b, i: (b, i, 0))
    ops = _ffn_operands(g_pre, g_post, w_up, conv_w, conv_b, w_down)
    return pl.pallas_call(
        _ffn_kernel,
        grid=(batch, seq // tm),
        in_specs=[tile] + [_resident(o.shape) for o in ops],
        out_specs=tile,
        out_shape=jax.ShapeDtypeStruct(x.shape, F32),
        scratch_shapes=_ffn_scratch(tm, d, w_down.shape[0], conv_w.shape[0]),
        compiler_params=_params("parallel", "arbitrary"),
        name="conv_ffn",
    )(x, *ops)


def kernel(x, norm_mix_pre, norm_mix_post, norm_ffn_pre, norm_ffn_post, rel_bias, attn_w_qkv, attn_w_o, lru_w_in, lru_conv_w, lru_conv_b, lru_ga_w, lru_ga_b, lru_gx_w, lru_gx_b, lru_lambda, lru_w_out, ffn_w_up, ffn_conv_w, ffn_conv_b, ffn_w_down):
    bias = _bias_tables(rel_bias)
    for layer in range(norm_mix_pre.shape[0]):
        j = layer // 2
        ffn = (norm_ffn_pre[layer], norm_ffn_post[layer], ffn_w_up[layer],
               ffn_conv_w[layer], ffn_conv_b[layer], ffn_w_down[layer])
        if layer % 2 == 0:
            x = _attention_layer(x, norm_mix_pre[layer], norm_mix_post[layer],
                                 attn_w_qkv[j], attn_w_o[j], bias)
        else:
            x = _lru_layer(x, norm_mix_pre[layer], norm_mix_post[layer], lru_w_in[j],
                           lru_conv_w[j], lru_conv_b[j], lru_ga_w[j], lru_ga_b[j],
                           lru_gx_w[j], lru_gx_b[j], lru_lambda[j], lru_w_out[j])
        x = _ffn_layer(x, *ffn)
    return x
```

```python
import functools

import numpy as np
import jax
import jax.numpy as jnp
from jax import lax
from jax.experimental import pallas as pl
from jax.experimental.pallas import tpu as pltpu

F32 = jnp.float32
BF16 = jnp.bfloat16

RMS_EPS = 1e-6
NEG_INF = -1e30
HEAD_DIM = 128
HEADS_PER_GROUP = 8
WINDOWS = (128, 512, 2048)
DILATIONS = (1, 4, 16)
N_GROUPS = 3
BAND = 128
NUM_BUCKETS = 32
MAX_DISTANCE = 2048
LRU_C = 8.0
LOG2E = float(np.log2(np.e))
Q_SCALE = HEAD_DIM ** -0.5 * LOG2E
LRU_BLOCKS = 4
SUBLANES = 8
LANES = 128
VMEM_LIMIT_BYTES = 56 * 1024 * 1024
ROW_TILE = 512
ATTN_TILE = 1024


def _params(*semantics):
    return pltpu.CompilerParams(dimension_semantics=semantics,
                                vmem_limit_bytes=VMEM_LIMIT_BYTES)


def _resident(shape):
    nd = len(shape)
    return pl.BlockSpec(shape, lambda *_: (0,) * nd, pipeline_mode=pl.Buffered(1))


def _rms(x, g):
    return x * lax.rsqrt(jnp.mean(x * x, axis=-1, keepdims=True) + RMS_EPS) * g


def _dot(a, b):
    return jnp.dot(a, b, preferred_element_type=F32)


def _gelu_tanh(x):
    c = float(np.sqrt(2.0 / np.pi))
    t = jnp.tanh(x * (c + (c * 0.044715) * (x * x)))
    return x * (0.5 + 0.5 * t)


def _sigmoid(x):
    return 0.5 + 0.5 * jnp.tanh(0.5 * x)


def _t5_bucket(dist):
    max_exact = NUM_BUCKETS // 2
    d = np.maximum(dist, 1).astype(np.float64)
    large = max_exact + (np.log(d / max_exact) / np.log(MAX_DISTANCE / max_exact)
                         * (NUM_BUCKETS - max_exact)).astype(np.int32)
    large = np.minimum(large, NUM_BUCKETS - 1)
    return np.where(dist < max_exact, dist, large).astype(np.int32)


def _bucket_tables():
    i = np.arange(BAND)[:, None]
    k = np.arange(2 * BAND)[None, :]
    m = i + BAND - k
    valid = (m >= 0) & (m <= BAND)
    tabs = [np.where(valid, _t5_bucket(np.clip(m, 0, BAND) * d), -1) for d in DILATIONS]
    return np.stack(tabs).astype(np.int32)


def _bias_kernel(tbl_ref, bucket_ref, out_ref):
    group = pl.program_id(0)
    bucket = bucket_ref[0]
    masked = jnp.where(bucket < 0, NEG_INF, 0.0).astype(F32)
    for hd in range(HEADS_PER_GROUP):
        acc = masked
        for j in range(NUM_BUCKETS):
            acc = jnp.where(bucket == j, tbl_ref[j, group * HEADS_PER_GROUP + hd], acc)
        out_ref[hd] = acc * LOG2E


def _bias_tables(rel_bias):
    n_heads = rel_bias.shape[1]
    buckets = jnp.asarray(_bucket_tables())
    return pl.pallas_call(
        _bias_kernel,
        grid=(n_heads // HEADS_PER_GROUP,),
        in_specs=[pl.BlockSpec(memory_space=pltpu.SMEM),
                  pl.BlockSpec((1, BAND, 2 * BAND), lambda g: (g, 0, 0))],
        out_specs=pl.BlockSpec((HEADS_PER_GROUP, BAND, 2 * BAND), lambda g: (g, 0, 0)),
        out_shape=jax.ShapeDtypeStruct((n_heads, BAND, 2 * BAND), F32),
        compiler_params=_params("arbitrary"),
        name="bias_tables",
    )(rel_bias.astype(F32), buckets)


GROUP_WIDTH = HEADS_PER_GROUP * HEAD_DIM
N_SLABS = GROUP_WIDTH // LANES


def _qkv_kernel(x_ref, g_ref, w_ref, o0_ref, o1_ref, o2_ref, hs_ref, hp_ref):
    tm = x_ref.shape[1]
    h = _rms(x_ref[0], g_ref[...])
    hb = h.astype(BF16)
    for c in range(3):
        cols = slice(c * GROUP_WIDTH, (c + 1) * GROUP_WIDTH)
        res = _dot(hb, w_ref[:, cols])
        o0_ref[0, 0, :, cols] = (res * Q_SCALE if c == 0 else res).astype(o0_ref.dtype)
    for j in range(N_SLABS):
        hs_ref[j] = h[:, j * LANES:(j + 1) * LANES]
    for group, o_ref in ((1, o1_ref), (2, o2_ref)):
        d = DILATIONS[group]
        n = tm // d
        for r in range(d):
            for j in range(N_SLABS):
                hp_ref[r * n:(r + 1) * n, j * LANES:(j + 1) * LANES] = (
                    hs_ref[j, pl.ds(r, n, stride=d), :].astype(BF16))
        hp = hp_ref[...]
        for c in range(3):
            cols = slice(c * GROUP_WIDTH, (c + 1) * GROUP_WIDTH)
            wcols = slice((3 * group + c) * GROUP_WIDTH, (3 * group + c + 1) * GROUP_WIDTH)
            res = _dot(hp, w_ref[:, wcols])
            res = (res * Q_SCALE if c == 0 else res).astype(o_ref.dtype)
            for r in range(d):
                o_ref[0, r, :, cols] = res[r * n:(r + 1) * n, :]


def _qkv_proj(x, g, w):
    batch, seq, d_model = x.shape
    tm = min(ROW_TILE, seq)
    out_specs, out_shape = [], []
    for d in DILATIONS:
        out_specs.append(pl.BlockSpec((1, d, tm // d, 3 * GROUP_WIDTH), lambda b, i: (b, 0, i, 0)))
        out_shape.append(jax.ShapeDtypeStruct((batch, d, seq // d, 3 * GROUP_WIDTH), BF16))
    return pl.pallas_call(
        _qkv_kernel,
        grid=(batch, seq // tm),
        in_specs=[pl.BlockSpec((1, tm, d_model), lambda b, i: (b, i, 0)),
                  _resident((1, d_model)), _resident(w.shape)],
        out_specs=out_specs,
        out_shape=out_shape,
        scratch_shapes=[pltpu.VMEM((N_SLABS, tm, LANES), F32),
                        pltpu.VMEM((tm, d_model), BF16)],
        compiler_params=_params("parallel", "parallel"),
        name="qkv_proj",
    )(x, g.reshape(1, d_model), w)


def _attn_kernel(qkv_ref, prev_ref, bias_ref, o_ref, lse_ref, kk_ref, vv_ref):
    n_streams, tq = qkv_ref.shape[1], qkv_ref.shape[2]
    k0, v0 = GROUP_WIDTH, 2 * GROUP_WIDTH
    first_tile = pl.program_id(2) == 0
    ones = jnp.ones((BAND + tq, HEAD_DIM), BF16)
    lane = lax.broadcasted_iota(jnp.int32, (BAND, LANES), 1)
    key_col = lax.broadcasted_iota(jnp.int32, (BAND, 2 * BAND), 1)
    for st in range(n_streams):
        kk_ref[st, 0:BAND, :] = prev_ref[0, st, :, k0:v0]
        kk_ref[st, BAND:, :] = qkv_ref[0, st, :, k0:v0]
        for h in range(HEADS_PER_GROUP):
            cols = slice(v0 + h * HEAD_DIM, v0 + (h + 1) * HEAD_DIM)
            vv_ref[st, 0:BAND, 2 * h * HEAD_DIM:(2 * h + 1) * HEAD_DIM] = prev_ref[0, st, :, cols]
            vv_ref[st, BAND:, 2 * h * HEAD_DIM:(2 * h + 1) * HEAD_DIM] = qkv_ref[0, st, :, cols]
            vv_ref[st, :, (2 * h + 1) * HEAD_DIM:(2 * h + 2) * HEAD_DIM] = ones
        for j in range(tq // BAND):
            rows = slice(j * BAND, (j + 1) * BAND)
            win = slice(j * BAND, (j + 2) * BAND)
            lse_tile = jnp.zeros((BAND, LANES), F32)
            for h in range(HEADS_PER_GROUP):
                cols = slice(h * HEAD_DIM, (h + 1) * HEAD_DIM)
                logits = lax.dot_general(qkv_ref[0, st, rows, cols], kk_ref[st, win, cols],
                                         (((1,), (1,)), ((), ())), preferred_element_type=F32)
                logits = logits + bias_ref[h]
                if j == 0:
                    logits = jnp.where(first_tile & (key_col < BAND), NEG_INF, logits)
                mx = jnp.max(logits, axis=-1, keepdims=True)
                p = jnp.exp2(logits - mx)
                pv = _dot(p.astype(BF16),
                          vv_ref[st, win, 2 * h * HEAD_DIM:(2 * h + 2) * HEAD_DIM])
                s = pv[:, HEAD_DIM:]
                o_ref[0, st, rows, cols] = (pv[:, :HEAD_DIM] / s).astype(o_ref.dtype)
                lse_tile = jnp.where(lane == h, mx + jnp.log2(s), lse_tile)
            lse_ref[0, st, rows, :] = lse_tile


def _group_attention(qkv, bias, group):
    batch, d, u, _ = qkv.shape
    tq = min(ATTN_TILE, u)
    ns = min(d, ATTN_TILE // tq)
    nblk = tq // BAND
    width = qkv.shape[3]
    return pl.pallas_call(
        _attn_kernel,
        grid=(batch, d // ns, u // tq),
        in_specs=[pl.BlockSpec((1, ns, tq, width), lambda b, r, i: (b, r, i, 0)),
                  pl.BlockSpec((1, ns, BAND, width),
                               lambda b, r, i: (b, r, jnp.maximum(i * nblk - 1, 0), 0)),
                  pl.BlockSpec((HEADS_PER_GROUP, BAND, 2 * BAND), lambda b, r, i: (group, 0, 0))],
        out_specs=[pl.BlockSpec((1, ns, tq, GROUP_WIDTH), lambda b, r, i: (b, r, i, 0)),
                   pl.BlockSpec((1, ns, tq, LANES), lambda b, r, i: (b, r, i, 0))],
        out_shape=[jax.ShapeDtypeStruct((batch, d, u, GROUP_WIDTH), BF16),
                   jax.ShapeDtypeStruct((batch, d, u, LANES), F32)],
        scratch_shapes=[pltpu.VMEM((ns, BAND + tq, GROUP_WIDTH), BF16),
                        pltpu.VMEM((ns, BAND + tq, 2 * GROUP_WIDTH), BF16)],
        compiler_params=_params("parallel", "parallel", "parallel"),
        name=f"attn_group{group}",
    )(qkv, qkv, bias)


def _stream_to_token_permutation(tm, d):
    n = tm // d
    t = np.arange(tm)
    perm = np.zeros((tm, tm), np.float32)
    perm[t, (t % d) * n + t // d] = 1.0
    return perm


def _combine_tile(x_ref, o0_ref, o1_ref, o2_ref, l0_ref, l1_ref, l2_ref, p1_ref, p2_ref,
                  w_ref, g_ref, lt_ref):
    tm = x_ref.shape[0]
    o_tok = []
    for gi, (o_ref, l_ref, p_ref) in enumerate(((o1_ref, l1_ref, p1_ref), (o2_ref, l2_ref, p2_ref))):
        d = DILATIONS[gi + 1]
        n = tm // d
        for r in range(d):
            lt_ref[gi, pl.ds(r, n, stride=d), :] = l_ref[0, r]
        o_tok.append(_dot(p_ref[...], o_ref[0].reshape(tm, GROUP_WIDTH)))
    lses = [l0_ref[0, 0], lt_ref[0], lt_ref[1]]
    mx = jnp.maximum(jnp.maximum(lses[0], lses[1]), lses[2])
    es = [jnp.exp2(l - mx) for l in lses]
    inv = 1.0 / (es[0] + es[1] + es[2])
    alphas = [e * inv for e in es[1:]]
    parts = []
    for h in range(HEADS_PER_GROUP):
        cols = slice(h * HEAD_DIM, (h + 1) * HEAD_DIM)
        base = o0_ref[0, 0, :, cols].astype(F32)
        acc = base
        for gi in range(N_GROUPS - 1):
            acc = acc + alphas[gi][:, h:h + 1] * (o_tok[gi][:, cols] - base)
        parts.append(acc.astype(BF16))
    o = jnp.concatenate(parts, axis=1)
    return x_ref[...] + _rms(_dot(o, w_ref[...]), g_ref[...])


def _combine_kernel(x_ref, o0_ref, o1_ref, o2_ref, l0_ref, l1_ref, l2_ref, p1_ref, p2_ref,
                    w_ref, g_ref, out_ref, lt_ref):
    out_ref[0] = _combine_tile(x_ref.at[0], o0_ref, o1_ref, o2_ref, l0_ref, l1_ref, l2_ref,
                               p1_ref, p2_ref, w_ref, g_ref, lt_ref)


def _combine_project(x, outs, lses, w_o, g_post):
    batch, seq, d_model = x.shape
    tm = min(ROW_TILE, seq)
    tile = pl.BlockSpec((1, tm, d_model), lambda b, i: (b, i, 0))
    stream = lambda d, width: pl.BlockSpec((1, d, tm // d, width), lambda b, i: (b, 0, i, 0))
    perms = [jnp.asarray(_stream_to_token_permutation(tm, d), BF16) for d in DILATIONS[1:]]
    return pl.pallas_call(
        _combine_kernel,
        grid=(batch, seq // tm),
        in_specs=[tile] + [stream(d, GROUP_WIDTH) for d in DILATIONS]
                 + [stream(d, LANES) for d in DILATIONS]
                 + [_resident(p.shape) for p in perms]
                 + [_resident(w_o.shape), _resident((1, d_model))],
        out_specs=tile,
        out_shape=jax.ShapeDtypeStruct(x.shape, F32),
        scratch_shapes=[pltpu.VMEM((N_GROUPS - 1, tm, LANES), F32)],
        compiler_params=_params("parallel", "parallel"),
        name="attn_combine_out",
    )(x, *outs, *lses, *perms, w_o, g_post.reshape(1, d_model))


def _attention_layer(x, g_pre, g_post, w_qkv, w_o, bias):
    qkvs = _qkv_proj(x, g_pre, w_qkv.astype(BF16))
    outs, lses = [], []
    for group in range(N_GROUPS):
        o, lse = _group_attention(qkvs[group], bias, group)
        outs.append(o)
        lses.append(lse)
    return _combine_project(x, outs, lses, w_o.astype(BF16), g_post)


SEGMENTS = SUBLANES
SEG_PITCH_PAD = 8


def _to_interleaved(h, stage_ref, dst_ref):
    tm, d = h.shape
    seg = tm // SEGMENTS
    pitch = seg + SEG_PITCH_PAD
    for s in range(SEGMENTS):
        for j in range(d // LANES):
            stage_ref[j, s * pitch:s * pitch + seg, :] = h[s * seg:(s + 1) * seg, j * LANES:(j + 1) * LANES]
    for i in range(0, seg, 2):
        for j in range(d // LANES):
            pair = [stage_ref[j, pl.ds(i + e, SEGMENTS, stride=pitch), :] for e in range(2)]
            dst_ref[i * SEGMENTS:(i + 2) * SEGMENTS, j * LANES:(j + 1) * LANES] = (
                jnp.concatenate(pair, axis=0).astype(dst_ref.dtype))


def _residual_from_interleaved(y, stage_ref, x_ref, out_ref):
    tm, d = y.shape
    seg = tm // SEGMENTS
    pitch = seg + SEG_PITCH_PAD
    for i in range(seg):
        for j in range(d // LANES):
            stage_ref[j, pl.ds(i, SEGMENTS, stride=pitch), :] = (
                y[i * SEGMENTS:(i + 1) * SEGMENTS, j * LANES:(j + 1) * LANES])
    for s in range(SEGMENTS):
        rows = slice(s * seg, (s + 1) * seg)
        for j in range(d // LANES):
            cols = slice(j * LANES, (j + 1) * LANES)
            out_ref[rows, cols] = x_ref[rows, cols] + stage_ref[j, s * pitch:s * pitch + seg, :]


def _stage_conv_input(u, dst_ref, dst_cols, halo_ref, halo_cols, taps):
    tm, width = u.shape
    seg = tm // SEGMENTS
    head = (taps - 1) * SEGMENTS
    row = lax.broadcasted_iota(jnp.int32, (SEGMENTS, width), 0)
    dst_ref[head:head + tm, dst_cols] = u
    for k in range(1, taps):
        grp = u[(seg - k) * SEGMENTS:(seg - k + 1) * SEGMENTS, :]
        before = jnp.where(row == 0, halo_ref[k - 1:k, halo_cols], pltpu.roll(grp, 1, 0))
        dst_ref[head - k * SEGMENTS:head - (k - 1) * SEGMENTS, dst_cols] = before
        halo_ref[k - 1:k, halo_cols] = grp[SEGMENTS - 1:SEGMENTS, :]


def _causal_conv(src_ref, src_cols, tm, w_ref, w_cols, bias):
    taps = w_ref.shape[0]
    out = bias
    for t in range(taps):
        start = t * SEGMENTS
        out = out + src_ref[start:start + tm, src_cols] * w_ref[t:t + 1, w_cols]
    return out


def _log_sigmoid(x):
    return jnp.minimum(x, 0.0) - jnp.log1p(jnp.exp(-jnp.abs(x)))


def _sublane_scan(q, e):
    row = lax.broadcasted_iota(jnp.int32, q.shape, 0)
    for s in (1, 2, 4):
        q_prev = jnp.where(row >= s, pltpu.roll(q, s, 0), 1.0)
        e_prev = jnp.where(row >= s, pltpu.roll(e, s, 0), 0.0)
        e = q * e_prev + e
        q = q * q_prev
    return e


def _lru_kernel(x_ref, gpre_ref, gpost_ref, win_ref, cw_ref, cb_ref, gaw_ref, gab_ref,
                gxw_ref, gxb_ref, lam_ref, wout_ref, out_ref,
                halo_ref, stage_ref, hp_ref, xs_ref, a_ref, b_ref, p_ref, h_ref):
    n_sub, tm, width = a_ref.shape
    seg = tm // SEGMENTS
    taps = cw_ref.shape[0]

    @pl.when(pl.program_id(1) == 0)
    def _():
        halo_ref[...] = jnp.zeros_like(halo_ref)
        h_ref[...] = jnp.zeros_like(h_ref)

    blk = width // LRU_BLOCKS
    row = lax.broadcasted_iota(jnp.int32, (SEGMENTS, blk), 0)
    log2_decay = (LRU_C * LOG2E) * _log_sigmoid(lam_ref[...])
    block_cols = [slice(n * blk, (n + 1) * blk) for n in range(LRU_BLOCKS)]
    items = [(t, n) for t in range(n_sub) for n in range(LRU_BLOCKS)]
    vals = {item: dict() for item in items}
    hn = {}
    y_parts = {t: [] for t in range(n_sub)}

    def head(t):
        rows = slice(t * tm, (t + 1) * tm)
        _to_interleaved(_rms(x_ref[0, rows, :], gpre_ref[...]), stage_ref.at[t], hp_ref.at[t])
        hn[t] = hp_ref[t]

    def tail(t):
        rows = slice(t * tm, (t + 1) * tm)
        y = functools.reduce(lambda acc, part: acc + part, y_parts[t])
        _residual_from_interleaved(_rms(y, gpost_ref[...]), stage_ref.at[t],
                                   x_ref.at[0, rows], out_ref.at[0, rows])

    def project(item):
        t, n = item
        vals[item]["u"] = _dot(hn[t], win_ref[:, block_cols[n]])
        vals[item]["g"] = _dot(hn[t], win_ref[:, width + n * blk:width + (n + 1) * blk])

    def conv(item):
        t, n = item
        cols = block_cols[n]
        _stage_conv_input(vals[item].pop("u"), xs_ref.at[t], cols, halo_ref, cols, taps)
        vals[item]["xc"] = _causal_conv(xs_ref.at[t], cols, tm, cw_ref, cols, cb_ref[:, cols])
        vals[item]["gate"] = _gelu_tanh(vals[item].pop("g"))

    def gates(item):
        n = item[1]
        xb = vals[item]["xc"].astype(BF16)
        vals[item]["r"] = _dot(xb, gaw_ref[n])
        vals[item]["i"] = _dot(xb, gxw_ref[n])

    def decay(item):
        t, n = item
        cols = block_cols[n]
        r = _sigmoid(vals[item].pop("r") + gab_ref[n:n + 1, :])
        gi = _sigmoid(vals[item].pop("i") + gxb_ref[n:n + 1, :])
        a = jnp.exp2(r * log2_decay[:, cols])
        a_ref[t, :, cols] = a
        b_ref[t, :, cols] = (jnp.exp2(0.5 * jnp.log2(jnp.maximum(1.0 - a * a, 0.0)))
                             * gi * vals[item].pop("xc"))

    def scan(item):
        t, n = item
        cols = block_cols[n]
        h = jnp.zeros((SEGMENTS, blk), F32)
        p = jnp.ones((SEGMENTS, blk), F32)
        for i in range(seg):
            rows = slice(i * SEGMENTS, (i + 1) * SEGMENTS)
            a_i = a_ref[t, rows, cols]
            h = a_i * h + b_ref[t, rows, cols]
            p = a_i * p
            b_ref[t, rows, cols] = h
            p_ref[t, rows, cols] = p
        e = jnp.where(row == 0, h_ref[:, cols], pltpu.roll(h, 1, 0))
        q = jnp.where(row == 0, 0.0, pltpu.roll(p, 1, 0))
        entry = _sublane_scan(q, e)
        h_ref[:, cols] = (h + p * entry)[SEGMENTS - 1:SEGMENTS, :]
        vals[item]["entry"] = entry

    def project_out(item):
        t, n = item
        cols = block_cols[n]
        hs = (b_ref[t, :, cols].reshape(seg, SEGMENTS, blk)
              + p_ref[t, :, cols].reshape(seg, SEGMENTS, blk) * vals[item].pop("entry")[None]
              ).reshape(tm, blk)
        y_parts[t].append(_dot((hs * vals[item].pop("gate")).astype(BF16), wout_ref[cols, :]))

    stages = (project, conv, gates, decay, scan, project_out)
    head(0)
    for slot in range(len(items) + len(stages) - 1):
        for k, stage in enumerate(stages):
            if 0 <= slot - k < len(items):
                item = items[slot - k]
                stage(item)
                if stage is project_out and item[1] == LRU_BLOCKS - 1:
                    tail(item[0])
        if slot == 0:
            for t in range(1, n_sub):
                head(t)


def _interleave_scratch(tm, d):
    seg = tm // SEGMENTS
    return [pltpu.VMEM((d // LANES, SEGMENTS * (seg + SEG_PITCH_PAD), LANES), F32),
            pltpu.VMEM((tm, d), BF16)]


def _lru_layer(x, g_pre, g_post, w_in, conv_w, conv_b, ga_w, ga_b, gx_w, gx_b, lam, w_out):
    batch, seq, d = x.shape
    width = w_out.shape[0]
    taps = conv_w.shape[0]
    tm = min(ROW_TILE, seq)
    n_sub = 2 if seq % (2 * tm) == 0 else 1
    seg = tm // SEGMENTS
    tile = pl.BlockSpec((1, n_sub * tm, d), lambda b, i: (b, i, 0))
    ops = [g_pre.reshape(1, d), g_post.reshape(1, d), w_in.astype(BF16), conv_w,
           conv_b.reshape(1, width), ga_w.astype(BF16), ga_b, gx_w.astype(BF16), gx_b,
           lam.reshape(1, width), w_out.astype(BF16)]
    return pl.pallas_call(
        _lru_kernel,
        grid=(batch, seq // (n_sub * tm)),
        in_specs=[tile] + [_resident(o.shape) for o in ops],
        out_specs=tile,
        out_shape=jax.ShapeDtypeStruct(x.shape, F32),
        scratch_shapes=[pltpu.VMEM((taps - 1, width), F32),
                        pltpu.VMEM((n_sub, d // LANES, SEGMENTS * (seg + SEG_PITCH_PAD), LANES), F32),
                        pltpu.VMEM((n_sub, tm, d), BF16),
                        pltpu.VMEM((n_sub, (taps - 1) * SEGMENTS + tm, width), F32),
                        pltpu.VMEM((n_sub, tm, width), F32),
                        pltpu.VMEM((n_sub, tm, width), F32),
                        pltpu.VMEM((n_sub, tm, width), F32),
                        pltpu.VMEM((1, width), F32)],
        compiler_params=_params("parallel", "arbitrary"),
        name="rglru_layer",
    )(x, *ops)


FFN_CHUNK = 1024


def _ffn_tile(x_ref, out_ref, gpre_ref, gpost_ref, wup_ref, cw_ref, cb_ref, wdn_ref,
              halo_ref, stage_ref, hp_ref, us_g, us_v, acc_ref):
    tm = acc_ref.shape[0]
    d_ff = wdn_ref.shape[0]
    taps = cw_ref.shape[0]
    _to_interleaved(_rms(x_ref[...], gpre_ref[...]), stage_ref, hp_ref)
    hn = hp_ref[...]
    n_chunks = d_ff // FFN_CHUNK
    y = None
    for c in range(n_chunks):
        halves = []
        for part, off in enumerate((c * FFN_CHUNK, d_ff + c * FFN_CHUNK)):
            src = slice(off, off + FFN_CHUNK)
            us_ref = (us_g, us_v)[part]
            _stage_conv_input(_dot(hn, wup_ref[:, src]), us_ref, slice(None), halo_ref, src, taps)
            halves.append(_causal_conv(us_ref, slice(None), tm, cw_ref, src, cb_ref[:, src]))
        act = (_gelu_tanh(halves[0]) * halves[1]).astype(BF16)
        contrib = _dot(act, wdn_ref[c * FFN_CHUNK:(c + 1) * FFN_CHUNK, :])
        if c == n_chunks - 1:
            y = contrib if c == 0 else acc_ref[...] + contrib
        elif c == 0:
            acc_ref[...] = contrib
        else:
            acc_ref[...] += contrib
    _residual_from_interleaved(_rms(y, gpost_ref[...]), stage_ref, x_ref, out_ref)


def _ffn_scratch(tm, d, d_ff, taps):
    return ([pltpu.VMEM((taps - 1, 2 * d_ff), F32)] + _interleave_scratch(tm, d)
            + [pltpu.VMEM(((taps - 1) * SEGMENTS + tm, FFN_CHUNK), F32)] * 2
            + [pltpu.VMEM((tm, d), F32)])


def _ffn_operands(g_pre, g_post, w_up, conv_w, conv_b, w_down):
    d, d_ff = w_down.shape[1], w_down.shape[0]
    return [g_pre.reshape(1, d), g_post.reshape(1, d), w_up.astype(BF16), conv_w,
            conv_b.reshape(1, 2 * d_ff), w_down.astype(BF16)]


def _ffn_kernel(x_ref, gpre_ref, gpost_ref, wup_ref, cw_ref, cb_ref, wdn_ref, out_ref,
                halo_ref, *scratch):
    @pl.when(pl.program_id(1) == 0)
    def _():
        halo_ref[...] = jnp.zeros_like(halo_ref)

    _ffn_tile(x_ref.at[0], out_ref.at[0], gpre_ref, gpost_ref, wup_ref, cw_ref, cb_ref, wdn_ref,
              halo_ref, *scratch)


def _ffn_layer(x, g_pre, g_post, w_up, conv_w, conv_b, w_down):
    batch, seq, d = x.shape
    tm = min(ROW_TILE, seq)
    tile = pl.BlockSpec((1, tm, d), lambda b, i: (b, i, 0))
    ops = _ffn_operands(g_pre, g_post, w_up, conv_w, conv_b, w_down)
    return pl.pallas_call(
        _ffn_kernel,
        grid=(batch, seq // tm),
        in_specs=[tile] + [_resident(o.shape) for o in ops],
        out_specs=tile,
        out_shape=jax.ShapeDtypeStruct(x.shape, F32),
        scratch_shapes=_ffn_scratch(tm, d, w_down.shape[0], conv_w.shape[0]),
        compiler_params=_params("parallel", "arbitrary"),
        name="conv_ffn",
    )(x, *ops)


def kernel(x, norm_mix_pre, norm_mix_post, norm_ffn_pre, norm_ffn_post, rel_bias, attn_w_qkv, attn_w_o, lru_w_in, lru_conv_w, lru_conv_b, lru_ga_w, lru_ga_b, lru_gx_w, lru_gx_b, lru_lambda, lru_w_out, ffn_w_up, ffn_conv_w, ffn_conv_b, ffn_w_down):
    bias = _bias_tables(rel_bias)
    for layer in range(norm_mix_pre.shape[0]):
        j = layer // 2
        ffn = (norm_ffn_pre[layer], norm_ffn_post[layer], ffn_w_up[layer],
               ffn_conv_w[layer], ffn_conv_b[layer], ffn_w_down[layer])
        if layer % 2 == 0:
            x = _attention_layer(x, norm_mix_pre[layer], norm_mix_post[layer],
                                 attn_w_qkv[j], attn_w_o[j], bias)
        else:
            x = _lru_layer(x, norm_mix_pre[layer], norm_mix_post[layer], lru_w_in[j],
                           lru_conv_w[j], lru_conv_b[j], lru_ga_w[j], lru_ga_b[j],
                           lru_gx_w[j], lru_gx_b[j], lru_lambda[j], lru_w_out[j])
        x = _ffn_layer(x, *ffn)
    return x
```

```python
import functools

import numpy as np
import jax
import jax.numpy as jnp
from jax import lax
from jax.experimental import pallas as pl
from jax.experimental.pallas import tpu as pltpu

F32 = jnp.float32
BF16 = jnp.bfloat16

RMS_EPS = 1e-6
NEG_INF = -1e30
HEAD_DIM = 128
HEADS_PER_GROUP = 8
WINDOWS = (128, 512, 2048)
DILATIONS = (1, 4, 16)
N_GROUPS = 3
BAND = 128
NUM_BUCKETS = 32
MAX_DISTANCE = 2048
LRU_C = 8.0
LOG2E = float(np.log2(np.e))
Q_SCALE = HEAD_DIM ** -0.5 * LOG2E
LRU_BLOCKS = 4
SUBLANES = 8
LANES = 128
VMEM_LIMIT_BYTES = 56 * 1024 * 1024
ROW_TILE = 512
ATTN_TILE = 2048
ATTN_MULTI_STREAM_TOKENS = 1024


def _params(*semantics):
    return pltpu.CompilerParams(dimension_semantics=semantics,
                                vmem_limit_bytes=VMEM_LIMIT_BYTES)


def _resident(shape):
    nd = len(shape)
    return pl.BlockSpec(shape, lambda *_: (0,) * nd, pipeline_mode=pl.Buffered(1))


def _rms(x, g):
    return x * lax.rsqrt(jnp.mean(x * x, axis=-1, keepdims=True) + RMS_EPS) * g


def _dot(a, b):
    return jnp.dot(a, b, preferred_element_type=F32)


def _gelu_tanh(x):
    c = float(np.sqrt(2.0 / np.pi))
    t = jnp.tanh(x * (c + (c * 0.044715) * (x * x)))
    return x * (0.5 + 0.5 * t)


def _sigmoid(x):
    return 0.5 + 0.5 * jnp.tanh(0.5 * x)


def _t5_bucket(dist):
    max_exact = NUM_BUCKETS // 2
    d = np.maximum(dist, 1).astype(np.float64)
    large = max_exact + (np.log(d / max_exact) / np.log(MAX_DISTANCE / max_exact)
                         * (NUM_BUCKETS - max_exact)).astype(np.int32)
    large = np.minimum(large, NUM_BUCKETS - 1)
    return np.where(dist < max_exact, dist, large).astype(np.int32)


def _bucket_tables():
    i = np.arange(BAND)[:, None]
    k = np.arange(2 * BAND)[None, :]
    m = i + BAND - k
    valid = (m >= 0) & (m <= BAND)
    tabs = [np.where(valid, _t5_bucket(np.clip(m, 0, BAND) * d), -1) for d in DILATIONS]
    return np.stack(tabs).astype(np.int32)


def _bias_kernel(tbl_ref, bucket_ref, out_ref):
    group = pl.program_id(0)
    bucket = bucket_ref[0]
    masked = jnp.where(bucket < 0, NEG_INF, 0.0).astype(F32)
    for hd in range(HEADS_PER_GROUP):
        acc = masked
        for j in range(NUM_BUCKETS):
            acc = jnp.where(bucket == j, tbl_ref[j, group * HEADS_PER_GROUP + hd], acc)
        out_ref[hd] = acc * LOG2E


def _bias_tables(rel_bias):
    n_heads = rel_bias.shape[1]
    buckets = jnp.asarray(_bucket_tables())
    return pl.pallas_call(
        _bias_kernel,
        grid=(n_heads // HEADS_PER_GROUP,),
        in_specs=[pl.BlockSpec(memory_space=pltpu.SMEM),
                  pl.BlockSpec((1, BAND, 2 * BAND), lambda g: (g, 0, 0))],
        out_specs=pl.BlockSpec((HEADS_PER_GROUP, BAND, 2 * BAND), lambda g: (g, 0, 0)),
        out_shape=jax.ShapeDtypeStruct((n_heads, BAND, 2 * BAND), F32),
        compiler_params=_params("arbitrary"),
        name="bias_tables",
    )(rel_bias.astype(F32), buckets)


GROUP_WIDTH = HEADS_PER_GROUP * HEAD_DIM
N_SLABS = GROUP_WIDTH // LANES


def _qkv_kernel(x_ref, g_ref, w_ref, o0_ref, o1_ref, o2_ref, hs_ref, hp_ref):
    tm = x_ref.shape[1]
    h = _rms(x_ref[0], g_ref[...])
    hb = h.astype(BF16)
    for c in range(3):
        cols = slice(c * GROUP_WIDTH, (c + 1) * GROUP_WIDTH)
        res = _dot(hb, w_ref[:, cols])
        o0_ref[0, 0, :, cols] = (res * Q_SCALE if c == 0 else res).astype(o0_ref.dtype)
    for j in range(N_SLABS):
        hs_ref[j] = h[:, j * LANES:(j + 1) * LANES]
    for group, o_ref in ((1, o1_ref), (2, o2_ref)):
        d = DILATIONS[group]
        n = tm // d
        for r in range(d):
            for j in range(N_SLABS):
                hp_ref[r * n:(r + 1) * n, j * LANES:(j + 1) * LANES] = (
                    hs_ref[j, pl.ds(r, n, stride=d), :].astype(BF16))
        hp = hp_ref[...]
        for c in range(3):
            cols = slice(c * GROUP_WIDTH, (c + 1) * GROUP_WIDTH)
            wcols = slice((3 * group + c) * GROUP_WIDTH, (3 * group + c + 1) * GROUP_WIDTH)
            res = _dot(hp, w_ref[:, wcols])
            res = (res * Q_SCALE if c == 0 else res).astype(o_ref.dtype)
            for r in range(d):
                o_ref[0, r, :, cols] = res[r * n:(r + 1) * n, :]


def _qkv_proj(x, g, w):
    batch, seq, d_model = x.shape
    tm = min(ROW_TILE, seq)
    out_specs, out_shape = [], []
    for d in DILATIONS:
        out_specs.append(pl.BlockSpec((1, d, tm // d, 3 * GROUP_WIDTH), lambda b, i: (b, 0, i, 0)))
        out_shape.append(jax.ShapeDtypeStruct((batch, d, seq // d, 3 * GROUP_WIDTH), BF16))
    return pl.pallas_call(
        _qkv_kernel,
        grid=(batch, seq // tm),
        in_specs=[pl.BlockSpec((1, tm, d_model), lambda b, i: (b, i, 0)),
                  _resident((1, d_model)), _resident(w.shape)],
        out_specs=out_specs,
        out_shape=out_shape,
        scratch_shapes=[pltpu.VMEM((N_SLABS, tm, LANES), F32),
                        pltpu.VMEM((tm, d_model), BF16)],
        compiler_params=_params("parallel", "parallel"),
        name="qkv_proj",
    )(x, g.reshape(1, d_model), w)


def _attn_kernel(qkv_ref, prev_ref, bias_ref, o_ref, lse_ref, kk_ref, vv_ref):
    n_streams, tq = qkv_ref.shape[1], qkv_ref.shape[2]
    k0, v0 = GROUP_WIDTH, 2 * GROUP_WIDTH
    first_tile = pl.program_id(2) == 0
    ones = jnp.ones((BAND + tq, HEAD_DIM), BF16)
    lane = lax.broadcasted_iota(jnp.int32, (BAND, LANES), 1)
    key_col = lax.broadcasted_iota(jnp.int32, (BAND, 2 * BAND), 1)
    for st in range(n_streams):
        kk_ref[st, 0:BAND, :] = prev_ref[0, st, :, k0:v0]
        kk_ref[st, BAND:, :] = qkv_ref[0, st, :, k0:v0]
        for h in range(HEADS_PER_GROUP):
            cols = slice(v0 + h * HEAD_DIM, v0 + (h + 1) * HEAD_DIM)
            vv_ref[st, 0:BAND, 2 * h * HEAD_DIM:(2 * h + 1) * HEAD_DIM] = prev_ref[0, st, :, cols]
            vv_ref[st, BAND:, 2 * h * HEAD_DIM:(2 * h + 1) * HEAD_DIM] = qkv_ref[0, st, :, cols]
            vv_ref[st, :, (2 * h + 1) * HEAD_DIM:(2 * h + 2) * HEAD_DIM] = ones
        for j in range(tq // BAND):
            rows = slice(j * BAND, (j + 1) * BAND)
            win = slice(j * BAND, (j + 2) * BAND)
            lse_tile = jnp.zeros((BAND, LANES), F32)
            for h in range(HEADS_PER_GROUP):
                cols = slice(h * HEAD_DIM, (h + 1) * HEAD_DIM)
                logits = lax.dot_general(qkv_ref[0, st, rows, cols], kk_ref[st, win, cols],
                                         (((1,), (1,)), ((), ())), preferred_element_type=F32)
                logits = logits + bias_ref[h]
                if j == 0:
                    logits = jnp.where(first_tile & (key_col < BAND), NEG_INF, logits)
                mx = jnp.max(logits, axis=-1, keepdims=True)
                p = jnp.exp2(logits - mx)
                pv = _dot(p.astype(BF16),
                          vv_ref[st, win, 2 * h * HEAD_DIM:(2 * h + 2) * HEAD_DIM])
                s = pv[:, HEAD_DIM:]
                o_ref[0, st, rows, cols] = (pv[:, :HEAD_DIM] / s).astype(o_ref.dtype)
                lse_tile = jnp.where(lane == h, mx + jnp.log2(s), lse_tile)
            lse_ref[0, st, rows, :] = lse_tile


def _group_attention(qkv, bias, group):
    batch, d, u, _ = qkv.shape
    tq = min(ATTN_TILE, u)
    ns = min(d, max(1, ATTN_MULTI_STREAM_TOKENS // tq))
    nblk = tq // BAND
    width = qkv.shape[3]
    return pl.pallas_call(
        _attn_kernel,
        grid=(batch, d // ns, u // tq),
        in_specs=[pl.BlockSpec((1, ns, tq, width), lambda b, r, i: (b, r, i, 0)),
                  pl.BlockSpec((1, ns, BAND, width),
                               lambda b, r, i: (b, r, jnp.maximum(i * nblk - 1, 0), 0)),
                  pl.BlockSpec((HEADS_PER_GROUP, BAND, 2 * BAND), lambda b, r, i: (group, 0, 0))],
        out_specs=[pl.BlockSpec((1, ns, tq, GROUP_WIDTH), lambda b, r, i: (b, r, i, 0)),
                   pl.BlockSpec((1, ns, tq, LANES), lambda b, r, i: (b, r, i, 0))],
        out_shape=[jax.ShapeDtypeStruct((batch, d, u, GROUP_WIDTH), BF16),
                   jax.ShapeDtypeStruct((batch, d, u, LANES), F32)],
        scratch_shapes=[pltpu.VMEM((ns, BAND + tq, GROUP_WIDTH), BF16),
                        pltpu.VMEM((ns, BAND + tq, 2 * GROUP_WIDTH), BF16)],
        compiler_params=_params("parallel", "parallel", "parallel"),
        name=f"attn_group{group}",
    )(qkv, qkv, bias)


def _stream_to_token_permutation(tm, d):
    n = tm // d
    t = np.arange(tm)
    perm = np.zeros((tm, tm), np.float32)
    perm[t, (t % d) * n + t // d] = 1.0
    return perm


def _combine_tile(x_ref, o0_ref, o1_ref, o2_ref, l0_ref, l1_ref, l2_ref, p1_ref, p2_ref,
                  w_ref, g_ref, lt_ref):
    tm = x_ref.shape[0]
    o_tok = []
    for gi, (o_ref, l_ref, p_ref) in enumerate(((o1_ref, l1_ref, p1_ref), (o2_ref, l2_ref, p2_ref))):
        d = DILATIONS[gi + 1]
        n = tm // d
        for r in range(d):
            lt_ref[gi, pl.ds(r, n, stride=d), :] = l_ref[0, r]
        o_tok.append(_dot(p_ref[...], o_ref[0].reshape(tm, GROUP_WIDTH)))
    lses = [l0_ref[0, 0], lt_ref[0], lt_ref[1]]
    mx = jnp.maximum(jnp.maximum(lses[0], lses[1]), lses[2])
    es = [jnp.exp2(l - mx) for l in lses]
    inv = 1.0 / (es[0] + es[1] + es[2])
    alphas = [e * inv for e in es[1:]]
    parts = []
    for h in range(HEADS_PER_GROUP):
        cols = slice(h * HEAD_DIM, (h + 1) * HEAD_DIM)
        base = o0_ref[0, 0, :, cols].astype(F32)
        acc = base
        for gi in range(N_GROUPS - 1):
            acc = acc + alphas[gi][:, h:h + 1] * (o_tok[gi][:, cols] - base)
        parts.append(acc.astype(BF16))
    o = jnp.concatenate(parts, axis=1)
    return x_ref[...] + _rms(_dot(o, w_ref[...]), g_ref[...])


def _combine_kernel(x_ref, o0_ref, o1_ref, o2_ref, l0_ref, l1_ref, l2_ref, p1_ref, p2_ref,
                    w_ref, g_ref, out_ref, lt_ref):
    out_ref[0] = _combine_tile(x_ref.at[0], o0_ref, o1_ref, o2_ref, l0_ref, l1_ref, l2_ref,
                               p1_ref, p2_ref, w_ref, g_ref, lt_ref)


def _combine_project(x, outs, lses, w_o, g_post):
    batch, seq, d_model = x.shape
    tm = min(ROW_TILE, seq)
    tile = pl.BlockSpec((1, tm, d_model), lambda b, i: (b, i, 0))
    stream = lambda d, width: pl.BlockSpec((1, d, tm // d, width), lambda b, i: (b, 0, i, 0))
    perms = [jnp.asarray(_stream_to_token_permutation(tm, d), BF16) for d in DILATIONS[1:]]
    return pl.pallas_call(
        _combine_kernel,
        grid=(batch, seq // tm),
        in_specs=[tile] + [stream(d, GROUP_WIDTH) for d in DILATIONS]
                 + [stream(d, LANES) for d in DILATIONS]
                 + [_resident(p.shape) for p in perms]
                 + [_resident(w_o.shape), _resident((1, d_model))],
        out_specs=tile,
        out_shape=jax.ShapeDtypeStruct(x.shape, F32),
        scratch_shapes=[pltpu.VMEM((N_GROUPS - 1, tm, LANES), F32)],
        compiler_params=_params("parallel", "parallel"),
        name="attn_combine_out",
    )(x, *outs, *lses, *perms, w_o, g_post.reshape(1, d_model))


def _attention_layer(x, g_pre, g_post, w_qkv, w_o, bias):
    qkvs = _qkv_proj(x, g_pre, w_qkv.astype(BF16))
    outs, lses = [], []
    for group in range(N_GROUPS):
        o, lse = _group_attention(qkvs[group], bias, group)
        outs.append(o)
        lses.append(lse)
    return _combine_project(x, outs, lses, w_o.astype(BF16), g_post)


SEGMENTS = SUBLANES
SEG_PITCH_PAD = 8


def _to_interleaved(h, stage_ref, dst_ref):
    tm, d = h.shape
    seg = tm // SEGMENTS
    pitch = seg + SEG_PITCH_PAD
    for s in range(SEGMENTS):
        for j in range(d // LANES):
            stage_ref[j, s * pitch:s * pitch + seg, :] = h[s * seg:(s + 1) * seg, j * LANES:(j + 1) * LANES]
    for i in range(0, seg, 2):
        for j in range(d // LANES):
            pair = [stage_ref[j, pl.ds(i + e, SEGMENTS, stride=pitch), :] for e in range(2)]
            dst_ref[i * SEGMENTS:(i + 2) * SEGMENTS, j * LANES:(j + 1) * LANES] = (
                jnp.concatenate(pair, axis=0).astype(dst_ref.dtype))


def _residual_from_interleaved(y, stage_ref, x_ref, out_ref):
    tm, d = y.shape
    seg = tm // SEGMENTS
    pitch = seg + SEG_PITCH_PAD
    for i in range(seg):
        for j in range(d // LANES):
            stage_ref[j, pl.ds(i, SEGMENTS, stride=pitch), :] = (
                y[i * SEGMENTS:(i + 1) * SEGMENTS, j * LANES:(j + 1) * LANES])
    for s in range(SEGMENTS):
        rows = slice(s * seg, (s + 1) * seg)
        for j in range(d // LANES):
            cols = slice(j * LANES, (j + 1) * LANES)
            out_ref[rows, cols] = x_ref[rows, cols] + stage_ref[j, s * pitch:s * pitch + seg, :]


def _stage_conv_input(u, dst_ref, dst_cols, halo_ref, halo_cols, taps):
    tm, width = u.shape
    seg = tm // SEGMENTS
    head = (taps - 1) * SEGMENTS
    row = lax.broadcasted_iota(jnp.int32, (SEGMENTS, width), 0)
    dst_ref[head:head + tm, dst_cols] = u
    for k in range(1, taps):
        grp = u[(seg - k) * SEGMENTS:(seg - k + 1) * SEGMENTS, :]
        before = jnp.where(row == 0, halo_ref[k - 1:k, halo_cols], pltpu.roll(grp, 1, 0))
        dst_ref[head - k * SEGMENTS:head - (k - 1) * SEGMENTS, dst_cols] = before
        halo_ref[k - 1:k, halo_cols] = grp[SEGMENTS - 1:SEGMENTS, :]


def _causal_conv(src_ref, src_cols, tm, w_ref, w_cols, bias):
    taps = w_ref.shape[0]
    out = bias
    for t in range(taps):
        start = t * SEGMENTS
        out = out + src_ref[start:start + tm, src_cols] * w_ref[t:t + 1, w_cols]
    return out


def _log_sigmoid(x):
    return jnp.minimum(x, 0.0) - jnp.log1p(jnp.exp(-jnp.abs(x)))


def _sublane_scan(q, e):
    row = lax.broadcasted_iota(jnp.int32, q.shape, 0)
    for s in (1, 2, 4):
        q_prev = jnp.where(row >= s, pltpu.roll(q, s, 0), 1.0)
        e_prev = jnp.where(row >= s, pltpu.roll(e, s, 0), 0.0)
        e = q * e_prev + e
        q = q * q_prev
    return e


def _lru_kernel(x_ref, gpre_ref, gpost_ref, win_ref, cw_ref, cb_ref, gaw_ref, gab_ref,
                gxw_ref, gxb_ref, lam_ref, wout_ref, out_ref,
                halo_ref, stage_ref, hp_ref, xs_ref, a_ref, b_ref, p_ref, h_ref):
    n_sub, tm, width = a_ref.shape
    seg = tm // SEGMENTS
    taps = cw_ref.shape[0]

    @pl.when(pl.program_id(1) == 0)
    def _():
        halo_ref[...] = jnp.zeros_like(halo_ref)
        h_ref[...] = jnp.zeros_like(h_ref)

    blk = width // LRU_BLOCKS
    row = lax.broadcasted_iota(jnp.int32, (SEGMENTS, blk), 0)
    log2_decay = (LRU_C * LOG2E) * _log_sigmoid(lam_ref[...])
    block_cols = [slice(n * blk, (n + 1) * blk) for n in range(LRU_BLOCKS)]
    items = [(t, n) for t in range(n_sub) for n in range(LRU_BLOCKS)]
    vals = {item: dict() for item in items}
    hn = {}
    y_parts = {t: [] for t in range(n_sub)}

    def head(t):
        rows = slice(t * tm, (t + 1) * tm)
        _to_interleaved(_rms(x_ref[0, rows, :], gpre_ref[...]), stage_ref.at[t], hp_ref.at[t])
        hn[t] = hp_ref[t]

    def tail(t):
        rows = slice(t * tm, (t + 1) * tm)
        y = functools.reduce(lambda acc, part: acc + part, y_parts[t])
        _residual_from_interleaved(_rms(y, gpost_ref[...]), stage_ref.at[t],
                                   x_ref.at[0, rows], out_ref.at[0, rows])

    def project(item):
        t, n = item
        vals[item]["u"] = _dot(hn[t], win_ref[:, block_cols[n]])
        vals[item]["g"] = _dot(hn[t], win_ref[:, width + n * blk:width + (n + 1) * blk])

    def conv(item):
        t, n = item
        cols = block_cols[n]
        _stage_conv_input(vals[item].pop("u"), xs_ref.at[t], cols, halo_ref, cols, taps)
        vals[item]["xc"] = _causal_conv(xs_ref.at[t], cols, tm, cw_ref, cols, cb_ref[:, cols])
        vals[item]["gate"] = _gelu_tanh(vals[item].pop("g"))

    def gates(item):
        n = item[1]
        xb = vals[item]["xc"].astype(BF16)
        vals[item]["r"] = _dot(xb, gaw_ref[n])
        vals[item]["i"] = _dot(xb, gxw_ref[n])

    def decay(item):
        t, n = item
        cols = block_cols[n]
        r = _sigmoid(vals[item].pop("r") + gab_ref[n:n + 1, :])
        gi = _sigmoid(vals[item].pop("i") + gxb_ref[n:n + 1, :])
        a = jnp.exp2(r * log2_decay[:, cols])
        a_ref[t, :, cols] = a
        b_ref[t, :, cols] = (jnp.exp2(0.5 * jnp.log2(jnp.maximum(1.0 - a * a, 0.0)))
                             * gi * vals[item].pop("xc"))

    def scan(item):
        t, n = item
        cols = block_cols[n]
        h = jnp.zeros((SEGMENTS, blk), F32)
        p = jnp.ones((SEGMENTS, blk), F32)
        for i in range(seg):
            rows = slice(i * SEGMENTS, (i + 1) * SEGMENTS)
            a_i = a_ref[t, rows, cols]
            h = a_i * h + b_ref[t, rows, cols]
            p = a_i * p
            b_ref[t, rows, cols] = h
            p_ref[t, rows, cols] = p
        e = jnp.where(row == 0, h_ref[:, cols], pltpu.roll(h, 1, 0))
        q = jnp.where(row == 0, 0.0, pltpu.roll(p, 1, 0))
        entry = _sublane_scan(q, e)
        h_ref[:, cols] = (h + p * entry)[SEGMENTS - 1:SEGMENTS, :]
        vals[item]["entry"] = entry

    def project_out(item):
        t, n = item
        cols = block_cols[n]
        hs = (b_ref[t, :, cols].reshape(seg, SEGMENTS, blk)
              + p_ref[t, :, cols].reshape(seg, SEGMENTS, blk) * vals[item].pop("entry")[None]
              ).reshape(tm, blk)
        y_parts[t].append(_dot((hs * vals[item].pop("gate")).astype(BF16), wout_ref[cols, :]))

    stages = (project, conv, gates, decay, scan, project_out)
    head(0)
    for slot in range(len(items) + len(stages) - 1):
        for k, stage in enumerate(stages):
            if 0 <= slot - k < len(items):
                item = items[slot - k]
                stage(item)
                if stage is project_out and item[1] == LRU_BLOCKS - 1:
                    tail(item[0])
        if slot == 0:
            for t in range(1, n_sub):
                head(t)


def _interleave_scratch(tm, d):
    seg = tm // SEGMENTS
    return [pltpu.VMEM((d // LANES, SEGMENTS * (seg + SEG_PITCH_PAD), LANES), F32),
            pltpu.VMEM((tm, d), BF16)]


def _lru_layer(x, g_pre, g_post, w_in, conv_w, conv_b, ga_w, ga_b, gx_w, gx_b, lam, w_out):
    batch, seq, d = x.shape
    width = w_out.shape[0]
    taps = conv_w.shape[0]
    tm = min(ROW_TILE, seq)
    n_sub = 2 if seq % (2 * tm) == 0 else 1
    seg = tm // SEGMENTS
    tile = pl.BlockSpec((1, n_sub * tm, d), lambda b, i: (b, i, 0))
    ops = [g_pre.reshape(1, d), g_post.reshape(1, d), w_in.astype(BF16), conv_w,
           conv_b.reshape(1, width), ga_w.astype(BF16), ga_b, gx_w.astype(BF16), gx_b,
           lam.reshape(1, width), w_out.astype(BF16)]
    return pl.pallas_call(
        _lru_kernel,
        grid=(batch, seq // (n_sub * tm)),
        in_specs=[tile] + [_resident(o.shape) for o in ops],
        out_specs=tile,
        out_shape=jax.ShapeDtypeStruct(x.shape, F32),
        scratch_shapes=[pltpu.VMEM((taps - 1, width), F32),
                        pltpu.VMEM((n_sub, d // LANES, SEGMENTS * (seg + SEG_PITCH_PAD), LANES), F32),
                        pltpu.VMEM((n_sub, tm, d), BF16),
                        pltpu.VMEM((n_sub, (taps - 1) * SEGMENTS + tm, width), F32),
                        pltpu.VMEM((n_sub, tm, width), F32),
                        pltpu.VMEM((n_sub, tm, width), F32),
                        pltpu.VMEM((n_sub, tm, width), F32),
                        pltpu.VMEM((1, width), F32)],
        compiler_params=_params("parallel", "arbitrary"),
        name="rglru_layer",
    )(x, *ops)


FFN_CHUNK = 1024


def _ffn_tile(x_ref, out_ref, gpre_ref, gpost_ref, wup_ref, cw_ref, cb_ref, wdn_ref,
              halo_ref, stage_ref, hp_ref, us_g, us_v, acc_ref):
    tm = acc_ref.shape[0]
    d_ff = wdn_ref.shape[0]
    taps = cw_ref.shape[0]
    _to_interleaved(_rms(x_ref[...], gpre_ref[...]), stage_ref, hp_ref)
    hn = hp_ref[...]
    n_chunks = d_ff // FFN_CHUNK
    y = None
    for c in range(n_chunks):
        halves = []
        for part, off in enumerate((c * FFN_CHUNK, d_ff + c * FFN_CHUNK)):
            src = slice(off, off + FFN_CHUNK)
            us_ref = (us_g, us_v)[part]
            _stage_conv_input(_dot(hn, wup_ref[:, src]), us_ref, slice(None), halo_ref, src, taps)
            halves.append(_causal_conv(us_ref, slice(None), tm, cw_ref, src, cb_ref[:, src]))
        act = (_gelu_tanh(halves[0]) * halves[1]).astype(BF16)
        contrib = _dot(act, wdn_ref[c * FFN_CHUNK:(c + 1) * FFN_CHUNK, :])
        if c == n_chunks - 1:
            y = contrib if c == 0 else acc_ref[...] + contrib
        elif c == 0:
            acc_ref[...] = contrib
        else:
            acc_ref[...] += contrib
    _residual_from_interleaved(_rms(y, gpost_ref[...]), stage_ref, x_ref, out_ref)


def _ffn_scratch(tm, d, d_ff, taps):
    return ([pltpu.VMEM((taps - 1, 2 * d_ff), F32)] + _interleave_scratch(tm, d)
            + [pltpu.VMEM(((taps - 1) * SEGMENTS + tm, FFN_CHUNK), F32)] * 2
            + [pltpu.VMEM((tm, d), F32)])


def _ffn_operands(g_pre, g_post, w_up, conv_w, conv_b, w_down):
    d, d_ff = w_down.shape[1], w_down.shape[0]
    return [g_pre.reshape(1, d), g_post.reshape(1, d), w_up.astype(BF16), conv_w,
            conv_b.reshape(1, 2 * d_ff), w_down.astype(BF16)]


def _ffn_kernel(x_ref, gpre_ref, gpost_ref, wup_ref, cw_ref, cb_ref, wdn_ref, out_ref,
                halo_ref, *scratch):
    @pl.when(pl.program_id(1) == 0)
    def _():
        halo_ref[...] = jnp.zeros_like(halo_ref)

    _ffn_tile(x_ref.at[0], out_ref.at[0], gpre_ref, gpost_ref, wup_ref, cw_ref, cb_ref, wdn_ref,
              halo_ref, *scratch)


def _ffn_layer(x, g_pre, g_post, w_up, conv_w, conv_b, w_down):
    batch, seq, d = x.shape
    tm = min(ROW_TILE, seq)
    tile = pl.BlockSpec((1, tm, d), lambda b, i: (b, i, 0))
    ops = _ffn_operands(g_pre, g_post, w_up, conv_w, conv_b, w_down)
    return pl.pallas_call(
        _ffn_kernel,
        grid=(batch, seq // tm),
        in_specs=[tile] + [_resident(o.shape) for o in ops],
        out_specs=tile,
        out_shape=jax.ShapeDtypeStruct(x.shape, F32),
        scratch_shapes=_ffn_scratch(tm, d, w_down.shape[0], conv_w.shape[0]),
        compiler_params=_params("parallel", "arbitrary"),
        name="conv_ffn",
    )(x, *ops)


def kernel(x, norm_mix_pre, norm_mix_post, norm_ffn_pre, norm_ffn_post, rel_bias, attn_w_qkv, attn_w_o, lru_w_in, lru_conv_w, lru_conv_b, lru_ga_w, lru_ga_b, lru_gx_w, lru_gx_b, lru_lambda, lru_w_out, ffn_w_up, ffn_conv_w, ffn_conv_b, ffn_w_down):
    bias = _bias_tables(rel_bias)
    for layer in range(norm_mix_pre.shape[0]):
        j = layer // 2
        ffn = (norm_ffn_pre[layer], norm_ffn_post[layer], ffn_w_up[layer],
               ffn_conv_w[layer], ffn_conv_b[layer], ffn_w_down[layer])
        if layer % 2 == 0:
            x = _attention_layer(x, norm_mix_pre[layer], norm_mix_post[layer],
                                 attn_w_qkv[j], attn_w_o[j], bias)
        else:
            x = _lru_layer(x, norm_mix_pre[layer], norm_mix_post[layer], lru_w_in[j],
                           lru_conv_w[j], lru_conv_b[j], lru_ga_w[j], lru_ga_b[j],
                           lru_gx_w[j], lru_gx_b[j], lru_lambda[j], lru_w_out[j])
        x = _ffn_layer(x, *ffn)
    return x
```

```python
import functools

import numpy as np
import jax
import jax.numpy as jnp
from jax import lax
from jax.experimental import pallas as pl
from jax.experimental.pallas import tpu as pltpu

F32 = jnp.float32
BF16 = jnp.bfloat16

RMS_EPS = 1e-6
NEG_INF = -1e30
HEAD_DIM = 128
HEADS_PER_GROUP = 8
WINDOWS = (128, 512, 2048)
DILATIONS = (1, 4, 16)
N_GROUPS = 3
BAND = 128
NUM_BUCKETS = 32
MAX_DISTANCE = 2048
LRU_C = 8.0
LOG2E = float(np.log2(np.e))
Q_SCALE = HEAD_DIM ** -0.5 * LOG2E
LRU_BLOCKS = 4
SUBLANES = 8
LANES = 128
VMEM_LIMIT_BYTES = 56 * 1024 * 1024
ROW_TILE = 512
ATTN_TILE = 2048
ATTN_MULTI_STREAM_TOKENS = 1024


def _params(*semantics, fuse_inputs=None):
    return pltpu.CompilerParams(dimension_semantics=semantics,
                                vmem_limit_bytes=VMEM_LIMIT_BYTES,
                                allow_input_fusion=fuse_inputs)


def _resident(shape):
    nd = len(shape)
    return pl.BlockSpec(shape, lambda *_: (0,) * nd, pipeline_mode=pl.Buffered(1))


def _rms(x, g):
    return x * lax.rsqrt(jnp.mean(x * x, axis=-1, keepdims=True) + RMS_EPS) * g


def _dot(a, b):
    return jnp.dot(a, b, preferred_element_type=F32)


def _gelu_tanh(x):
    c = float(np.sqrt(2.0 / np.pi))
    t = jnp.tanh(x * (c + (c * 0.044715) * (x * x)))
    return x * (0.5 + 0.5 * t)


def _sigmoid(x):
    return 0.5 + 0.5 * jnp.tanh(0.5 * x)


def _t5_bucket(dist):
    max_exact = NUM_BUCKETS // 2
    d = np.maximum(dist, 1).astype(np.float64)
    large = max_exact + (np.log(d / max_exact) / np.log(MAX_DISTANCE / max_exact)
                         * (NUM_BUCKETS - max_exact)).astype(np.int32)
    large = np.minimum(large, NUM_BUCKETS - 1)
    return np.where(dist < max_exact, dist, large).astype(np.int32)


def _bucket_tables():
    i = np.arange(BAND)[:, None]
    k = np.arange(2 * BAND)[None, :]
    m = i + BAND - k
    valid = (m >= 0) & (m <= BAND)
    tabs = [np.where(valid, _t5_bucket(np.clip(m, 0, BAND) * d), -1) for d in DILATIONS]
    return np.stack(tabs).astype(np.int32)


def _bias_kernel(tbl_ref, bucket_ref, out_ref):
    group = pl.program_id(0)
    bucket = bucket_ref[0]
    masked = jnp.where(bucket < 0, NEG_INF, 0.0).astype(F32)
    for hd in range(HEADS_PER_GROUP):
        acc = masked
        for j in range(NUM_BUCKETS):
            acc = jnp.where(bucket == j, tbl_ref[j, group * HEADS_PER_GROUP + hd], acc)
        out_ref[hd] = acc * LOG2E


def _bias_tables(rel_bias):
    n_heads = rel_bias.shape[1]
    buckets = jnp.asarray(_bucket_tables())
    return pl.pallas_call(
        _bias_kernel,
        grid=(n_heads // HEADS_PER_GROUP,),
        in_specs=[pl.BlockSpec(memory_space=pltpu.SMEM),
                  pl.BlockSpec((1, BAND, 2 * BAND), lambda g: (g, 0, 0))],
        out_specs=pl.BlockSpec((HEADS_PER_GROUP, BAND, 2 * BAND), lambda g: (g, 0, 0)),
        out_shape=jax.ShapeDtypeStruct((n_heads, BAND, 2 * BAND), F32),
        compiler_params=_params("arbitrary"),
        name="bias_tables",
    )(rel_bias.astype(F32), buckets)


GROUP_WIDTH = HEADS_PER_GROUP * HEAD_DIM
N_SLABS = GROUP_WIDTH // LANES


def _qkv_kernel(x_ref, g_ref, w_ref, o0_ref, o1_ref, o2_ref, hs_ref, hp_ref):
    tm = x_ref.shape[1]
    h = _rms(x_ref[0], g_ref[...])
    hb = h.astype(BF16)
    for c in range(3):
        cols = slice(c * GROUP_WIDTH, (c + 1) * GROUP_WIDTH)
        res = _dot(hb, w_ref[:, cols])
        o0_ref[0, 0, :, cols] = (res * Q_SCALE if c == 0 else res).astype(o0_ref.dtype)
    for j in range(N_SLABS):
        hs_ref[j] = h[:, j * LANES:(j + 1) * LANES]
    for group, o_ref in ((1, o1_ref), (2, o2_ref)):
        d = DILATIONS[group]
        n = tm // d
        for r in range(d):
            for j in range(N_SLABS):
                hp_ref[r * n:(r + 1) * n, j * LANES:(j + 1) * LANES] = (
                    hs_ref[j, pl.ds(r, n, stride=d), :].astype(BF16))
        hp = hp_ref[...]
        for c in range(3):
            cols = slice(c * GROUP_WIDTH, (c + 1) * GROUP_WIDTH)
            wcols = slice((3 * group + c) * GROUP_WIDTH, (3 * group + c + 1) * GROUP_WIDTH)
            res = _dot(hp, w_ref[:, wcols])
            res = (res * Q_SCALE if c == 0 else res).astype(o_ref.dtype)
            for r in range(d):
                o_ref[0, r, :, cols] = res[r * n:(r + 1) * n, :]


def _qkv_proj(x, g, w):
    batch, seq, d_model = x.shape
    tm = min(ROW_TILE, seq)
    out_specs, out_shape = [], []
    for d in DILATIONS:
        out_specs.append(pl.BlockSpec((1, d, tm // d, 3 * GROUP_WIDTH), lambda b, i: (b, 0, i, 0)))
        out_shape.append(jax.ShapeDtypeStruct((batch, d, seq // d, 3 * GROUP_WIDTH), BF16))
    return pl.pallas_call(
        _qkv_kernel,
        grid=(batch, seq // tm),
        in_specs=[pl.BlockSpec((1, tm, d_model), lambda b, i: (b, i, 0)),
                  _resident((1, d_model)), _resident(w.shape)],
        out_specs=out_specs,
        out_shape=out_shape,
        scratch_shapes=[pltpu.VMEM((N_SLABS, tm, LANES), F32),
                        pltpu.VMEM((tm, d_model), BF16)],
        compiler_params=_params("parallel", "parallel", fuse_inputs=[False, False, True]),
        name="qkv_proj",
    )(x, g.reshape(1, d_model), w)


def _attn_kernel(qkv_ref, prev_ref, bias_ref, o_ref, lse_ref, kk_ref, vv_ref):
    n_streams, tq = qkv_ref.shape[1], qkv_ref.shape[2]
    k0, v0 = GROUP_WIDTH, 2 * GROUP_WIDTH
    first_tile = pl.program_id(2) == 0
    ones = jnp.ones((BAND + tq, HEAD_DIM), BF16)
    lane = lax.broadcasted_iota(jnp.int32, (BAND, LANES), 1)
    key_col = lax.broadcasted_iota(jnp.int32, (BAND, 2 * BAND), 1)
    for st in range(n_streams):
        kk_ref[st, 0:BAND, :] = prev_ref[0, st, :, k0:v0]
        kk_ref[st, BAND:, :] = qkv_ref[0, st, :, k0:v0]
        for h in range(HEADS_PER_GROUP):
            cols = slice(v0 + h * HEAD_DIM, v0 + (h + 1) * HEAD_DIM)
            vv_ref[st, 0:BAND, 2 * h * HEAD_DIM:(2 * h + 1) * HEAD_DIM] = prev_ref[0, st, :, cols]
            vv_ref[st, BAND:, 2 * h * HEAD_DIM:(2 * h + 1) * HEAD_DIM] = qkv_ref[0, st, :, cols]
            vv_ref[st, :, (2 * h + 1) * HEAD_DIM:(2 * h + 2) * HEAD_DIM] = ones
        for j in range(tq // BAND):
            rows = slice(j * BAND, (j + 1) * BAND)
            win = slice(j * BAND, (j + 2) * BAND)
            lse_tile = jnp.zeros((BAND, LANES), F32)
            for h in range(HEADS_PER_GROUP):
                cols = slice(h * HEAD_DIM, (h + 1) * HEAD_DIM)
                logits = lax.dot_general(qkv_ref[0, st, rows, cols], kk_ref[st, win, cols],
                                         (((1,), (1,)), ((), ())), preferred_element_type=F32)
                logits = logits + bias_ref[h]
                if j == 0:
                    logits = jnp.where(first_tile & (key_col < BAND), NEG_INF, logits)
                mx = jnp.max(logits, axis=-1, keepdims=True)
                p = jnp.exp2(logits - mx)
                pv = _dot(p.astype(BF16),
                          vv_ref[st, win, 2 * h * HEAD_DIM:(2 * h + 2) * HEAD_DIM])
                s = pv[:, HEAD_DIM:]
                o_ref[0, st, rows, cols] = (pv[:, :HEAD_DIM] / s).astype(o_ref.dtype)
                lse_tile = jnp.where(lane == h, mx + jnp.log2(s), lse_tile)
            lse_ref[0, st, rows, :] = lse_tile


def _group_attention(qkv, bias, group):
    batch, d, u, _ = qkv.shape
    tq = min(ATTN_TILE, u)
    ns = min(d, max(1, ATTN_MULTI_STREAM_TOKENS // tq))
    nblk = tq // BAND
    width = qkv.shape[3]
    return pl.pallas_call(
        _attn_kernel,
        grid=(batch, d // ns, u // tq),
        in_specs=[pl.BlockSpec((1, ns, tq, width), lambda b, r, i: (b, r, i, 0)),
                  pl.BlockSpec((1, ns, BAND, width),
                               lambda b, r, i: (b, r, jnp.maximum(i * nblk - 1, 0), 0)),
                  pl.BlockSpec((HEADS_PER_GROUP, BAND, 2 * BAND), lambda b, r, i: (group, 0, 0))],
        out_specs=[pl.BlockSpec((1, ns, tq, GROUP_WIDTH), lambda b, r, i: (b, r, i, 0)),
                   pl.BlockSpec((1, ns, tq, LANES), lambda b, r, i: (b, r, i, 0))],
        out_shape=[jax.ShapeDtypeStruct((batch, d, u, GROUP_WIDTH), BF16),
                   jax.ShapeDtypeStruct((batch, d, u, LANES), F32)],
        scratch_shapes=[pltpu.VMEM((ns, BAND + tq, GROUP_WIDTH), BF16),
                        pltpu.VMEM((ns, BAND + tq, 2 * GROUP_WIDTH), BF16)],
        compiler_params=_params("parallel", "parallel", "parallel"),
        name=f"attn_group{group}",
    )(qkv, qkv, bias)


def _stream_to_token_permutation(tm, d):
    n = tm // d
    t = np.arange(tm)
    perm = np.zeros((tm, tm), np.float32)
    perm[t, (t % d) * n + t // d] = 1.0
    return perm


def _combine_tile(x_ref, o0_ref, o1_ref, o2_ref, l0_ref, l1_ref, l2_ref, p1_ref, p2_ref,
                  w_ref, g_ref, lt_ref):
    tm = x_ref.shape[0]
    o_tok = []
    for gi, (o_ref, l_ref, p_ref) in enumerate(((o1_ref, l1_ref, p1_ref), (o2_ref, l2_ref, p2_ref))):
        d = DILATIONS[gi + 1]
        n = tm // d
        for r in range(d):
            lt_ref[gi, pl.ds(r, n, stride=d), :] = l_ref[0, r]
        o_tok.append(_dot(p_ref[...], o_ref[0].reshape(tm, GROUP_WIDTH)))
    lses = [l0_ref[0, 0], lt_ref[0], lt_ref[1]]
    mx = jnp.maximum(jnp.maximum(lses[0], lses[1]), lses[2])
    es = [jnp.exp2(l - mx) for l in lses]
    inv = 1.0 / (es[0] + es[1] + es[2])
    alphas = [e * inv for e in es[1:]]
    parts = []
    for h in range(HEADS_PER_GROUP):
        cols = slice(h * HEAD_DIM, (h + 1) * HEAD_DIM)
        base = o0_ref[0, 0, :, cols].astype(F32)
        acc = base
        for gi in range(N_GROUPS - 1):
            acc = acc + alphas[gi][:, h:h + 1] * (o_tok[gi][:, cols] - base)
        parts.append(acc.astype(BF16))
    o = jnp.concatenate(parts, axis=1)
    return x_ref[...] + _rms(_dot(o, w_ref[...]), g_ref[...])


def _combine_kernel(x_ref, o0_ref, o1_ref, o2_ref, l0_ref, l1_ref, l2_ref, p1_ref, p2_ref,
                    w_ref, g_ref, out_ref, lt_ref):
    out_ref[0] = _combine_tile(x_ref.at[0], o0_ref, o1_ref, o2_ref, l0_ref, l1_ref, l2_ref,
                               p1_ref, p2_ref, w_ref, g_ref, lt_ref)


def _combine_project(x, outs, lses, w_o, g_post):
    batch, seq, d_model = x.shape
    tm = min(ROW_TILE, seq)
    tile = pl.BlockSpec((1, tm, d_model), lambda b, i: (b, i, 0))
    stream = lambda d, width: pl.BlockSpec((1, d, tm // d, width), lambda b, i: (b, 0, i, 0))
    perms = [jnp.asarray(_stream_to_token_permutation(tm, d), BF16) for d in DILATIONS[1:]]
    return pl.pallas_call(
        _combine_kernel,
        grid=(batch, seq // tm),
        in_specs=[tile] + [stream(d, GROUP_WIDTH) for d in DILATIONS]
                 + [stream(d, LANES) for d in DILATIONS]
                 + [_resident(p.shape) for p in perms]
                 + [_resident(w_o.shape), _resident((1, d_model))],
        out_specs=tile,
        out_shape=jax.ShapeDtypeStruct(x.shape, F32),
        scratch_shapes=[pltpu.VMEM((N_GROUPS - 1, tm, LANES), F32)],
        compiler_params=_params("parallel", "parallel"),
        name="attn_combine_out",
    )(x, *outs, *lses, *perms, w_o, g_post.reshape(1, d_model))


def _attention_layer(x, g_pre, g_post, w_qkv, w_o, bias):
    qkvs = _qkv_proj(x, g_pre, w_qkv.astype(BF16))
    outs, lses = [], []
    for group in range(N_GROUPS):
        o, lse = _group_attention(qkvs[group], bias, group)
        outs.append(o)
        lses.append(lse)
    return _combine_project(x, outs, lses, w_o.astype(BF16), g_post)


SEGMENTS = SUBLANES
SEG_PITCH_PAD = 8


def _to_interleaved(h, stage_ref, dst_ref):
    tm, d = h.shape
    seg = tm // SEGMENTS
    pitch = seg + SEG_PITCH_PAD
    for s in range(SEGMENTS):
        for j in range(d // LANES):
            stage_ref[j, s * pitch:s * pitch + seg, :] = h[s * seg:(s + 1) * seg, j * LANES:(j + 1) * LANES]
    for i in range(0, seg, 2):
        for j in range(d // LANES):
            pair = [stage_ref[j, pl.ds(i + e, SEGMENTS, stride=pitch), :] for e in range(2)]
            dst_ref[i * SEGMENTS:(i + 2) * SEGMENTS, j * LANES:(j + 1) * LANES] = (
                jnp.concatenate(pair, axis=0).astype(dst_ref.dtype))


def _residual_from_interleaved(y, stage_ref, x_ref, out_ref):
    tm, d = y.shape
    seg = tm // SEGMENTS
    pitch = seg + SEG_PITCH_PAD
    for i in range(seg):
        for j in range(d // LANES):
            stage_ref[j, pl.ds(i, SEGMENTS, stride=pitch), :] = (
                y[i * SEGMENTS:(i + 1) * SEGMENTS, j * LANES:(j + 1) * LANES])
    for s in range(SEGMENTS):
        rows = slice(s * seg, (s + 1) * seg)
        for j in range(d // LANES):
            cols = slice(j * LANES, (j + 1) * LANES)
            out_ref[rows, cols] = x_ref[rows, cols] + stage_ref[j, s * pitch:s * pitch + seg, :]


def _stage_conv_input(u, dst_ref, dst_cols, halo_ref, halo_cols, taps):
    tm, width = u.shape
    seg = tm // SEGMENTS
    head = (taps - 1) * SEGMENTS
    row = lax.broadcasted_iota(jnp.int32, (SEGMENTS, width), 0)
    dst_ref[head:head + tm, dst_cols] = u
    for k in range(1, taps):
        grp = u[(seg - k) * SEGMENTS:(seg - k + 1) * SEGMENTS, :]
        before = jnp.where(row == 0, halo_ref[k - 1:k, halo_cols], pltpu.roll(grp, 1, 0))
        dst_ref[head - k * SEGMENTS:head - (k - 1) * SEGMENTS, dst_cols] = before
        halo_ref[k - 1:k, halo_cols] = grp[SEGMENTS - 1:SEGMENTS, :]


def _causal_conv(src_ref, src_cols, tm, w_ref, w_cols, bias):
    taps = w_ref.shape[0]
    out = bias
    for t in range(taps):
        start = t * SEGMENTS
        out = out + src_ref[start:start + tm, src_cols] * w_ref[t:t + 1, w_cols]
    return out


def _log_sigmoid(x):
    return jnp.minimum(x, 0.0) - jnp.log1p(jnp.exp(-jnp.abs(x)))


def _sublane_scan(q, e):
    row = lax.broadcasted_iota(jnp.int32, q.shape, 0)
    for s in (1, 2, 4):
        q_prev = jnp.where(row >= s, pltpu.roll(q, s, 0), 1.0)
        e_prev = jnp.where(row >= s, pltpu.roll(e, s, 0), 0.0)
        e = q * e_prev + e
        q = q * q_prev
    return e


def _lru_kernel(x_ref, gpre_ref, gpost_ref, win_ref, cw_ref, cb_ref, gaw_ref, gab_ref,
                gxw_ref, gxb_ref, lam_ref, wout_ref, out_ref,
                halo_ref, stage_ref, hp_ref, xs_ref, a_ref, b_ref, p_ref, h_ref):
    n_sub, tm, width = a_ref.shape
    seg = tm // SEGMENTS
    taps = cw_ref.shape[0]

    @pl.when(pl.program_id(1) == 0)
    def _():
        halo_ref[...] = jnp.zeros_like(halo_ref)
        h_ref[...] = jnp.zeros_like(h_ref)

    blk = width // LRU_BLOCKS
    row = lax.broadcasted_iota(jnp.int32, (SEGMENTS, blk), 0)
    log2_decay = (LRU_C * LOG2E) * _log_sigmoid(lam_ref[...])
    block_cols = [slice(n * blk, (n + 1) * blk) for n in range(LRU_BLOCKS)]
    items = [(t, n) for t in range(n_sub) for n in range(LRU_BLOCKS)]
    vals = {item: dict() for item in items}
    hn = {}
    y_parts = {t: [] for t in range(n_sub)}

    def head(t):
        rows = slice(t * tm, (t + 1) * tm)
        _to_interleaved(_rms(x_ref[0, rows, :], gpre_ref[...]), stage_ref.at[t], hp_ref.at[t])
        hn[t] = hp_ref[t]

    def tail(t):
        rows = slice(t * tm, (t + 1) * tm)
        y = functools.reduce(lambda acc, part: acc + part, y_parts[t])
        _residual_from_interleaved(_rms(y, gpost_ref[...]), stage_ref.at[t],
                                   x_ref.at[0, rows], out_ref.at[0, rows])

    def project(item):
        t, n = item
        vals[item]["u"] = _dot(hn[t], win_ref[:, block_cols[n]])
        vals[item]["g"] = _dot(hn[t], win_ref[:, width + n * blk:width + (n + 1) * blk])

    def conv(item):
        t, n = item
        cols = block_cols[n]
        _stage_conv_input(vals[item].pop("u"), xs_ref.at[t], cols, halo_ref, cols, taps)
        vals[item]["xc"] = _causal_conv(xs_ref.at[t], cols, tm, cw_ref, cols, cb_ref[:, cols])
        vals[item]["gate"] = _gelu_tanh(vals[item].pop("g"))

    def gates(item):
        n = item[1]
        xb = vals[item]["xc"].astype(BF16)
        vals[item]["r"] = _dot(xb, gaw_ref[n])
        vals[item]["i"] = _dot(xb, gxw_ref[n])

    def decay(item):
        t, n = item
        cols = block_cols[n]
        r = _sigmoid(vals[item].pop("r") + gab_ref[n:n + 1, :])
        gi = _sigmoid(vals[item].pop("i") + gxb_ref[n:n + 1, :])
        a = jnp.exp2(r * log2_decay[:, cols])
        a_ref[t, :, cols] = a
        b_ref[t, :, cols] = (jnp.exp2(0.5 * jnp.log2(jnp.maximum(1.0 - a * a, 0.0)))
                             * gi * vals[item].pop("xc"))

    def scan(item):
        t, n = item
        cols = block_cols[n]
        h = jnp.zeros((SEGMENTS, blk), F32)
        p = jnp.ones((SEGMENTS, blk), F32)
        for i in range(seg):
            rows = slice(i * SEGMENTS, (i + 1) * SEGMENTS)
            a_i = a_ref[t, rows, cols]
            h = a_i * h + b_ref[t, rows, cols]
            p = a_i * p
            b_ref[t, rows, cols] = h
            p_ref[t, rows, cols] = p
        e = jnp.where(row == 0, h_ref[:, cols], pltpu.roll(h, 1, 0))
        q = jnp.where(row == 0, 0.0, pltpu.roll(p, 1, 0))
        entry = _sublane_scan(q, e)
        h_ref[:, cols] = (h + p * entry)[SEGMENTS - 1:SEGMENTS, :]
        vals[item]["entry"] = entry

    def project_out(item):
        t, n = item
        cols = block_cols[n]
        hs = (b_ref[t, :, cols].reshape(seg, SEGMENTS, blk)
              + p_ref[t, :, cols].reshape(seg, SEGMENTS, blk) * vals[item].pop("entry")[None]
              ).reshape(tm, blk)
        y_parts[t].append(_dot((hs * vals[item].pop("gate")).astype(BF16), wout_ref[cols, :]))

    stages = (project, conv, gates, decay, scan, project_out)
    head(0)
    for slot in range(len(items) + len(stages) - 1):
        for k, stage in enumerate(stages):
            if 0 <= slot - k < len(items):
                item = items[slot - k]
                stage(item)
                if stage is project_out and item[1] == LRU_BLOCKS - 1:
                    tail(item[0])
        if slot == 0:
            for t in range(1, n_sub):
                head(t)


def _interleave_scratch(tm, d):
    seg = tm // SEGMENTS
    return [pltpu.VMEM((d // LANES, SEGMENTS * (seg + SEG_PITCH_PAD), LANES), F32),
            pltpu.VMEM((tm, d), BF16)]


def _lru_layer(x, g_pre, g_post, w_in, conv_w, conv_b, ga_w, ga_b, gx_w, gx_b, lam, w_out):
    batch, seq, d = x.shape
    width = w_out.shape[0]
    taps = conv_w.shape[0]
    tm = min(ROW_TILE, seq)
    n_sub = 2 if seq % (2 * tm) == 0 else 1
    seg = tm // SEGMENTS
    tile = pl.BlockSpec((1, n_sub * tm, d), lambda b, i: (b, i, 0))
    ops = [g_pre.reshape(1, d), g_post.reshape(1, d), w_in.astype(BF16), conv_w,
           conv_b.reshape(1, width), ga_w.astype(BF16), ga_b, gx_w.astype(BF16), gx_b,
           lam.reshape(1, width), w_out.astype(BF16)]
    return pl.pallas_call(
        _lru_kernel,
        grid=(batch, seq // (n_sub * tm)),
        in_specs=[tile] + [_resident(o.shape) for o in ops],
        out_specs=tile,
        out_shape=jax.ShapeDtypeStruct(x.shape, F32),
        scratch_shapes=[pltpu.VMEM((taps - 1, width), F32),
                        pltpu.VMEM((n_sub, d // LANES, SEGMENTS * (seg + SEG_PITCH_PAD), LANES), F32),
                        pltpu.VMEM((n_sub, tm, d), BF16),
                        pltpu.VMEM((n_sub, (taps - 1) * SEGMENTS + tm, width), F32),
                        pltpu.VMEM((n_sub, tm, width), F32),
                        pltpu.VMEM((n_sub, tm, width), F32),
                        pltpu.VMEM((n_sub, tm, width), F32),
                        pltpu.VMEM((1, width), F32)],
        compiler_params=_params("parallel", "arbitrary"),
        name="rglru_layer",
    )(x, *ops)


FFN_CHUNK = 1024


def _ffn_tile(x_ref, out_ref, gpre_ref, gpost_ref, wup_ref, cw_ref, cb_ref, wdn_ref,
              halo_ref, stage_ref, hp_ref, us_g, us_v, acc_ref):
    tm = acc_ref.shape[0]
    d_ff = wdn_ref.shape[0]
    taps = cw_ref.shape[0]
    _to_interleaved(_rms(x_ref[...], gpre_ref[...]), stage_ref, hp_ref)
    hn = hp_ref[...]
    n_chunks = d_ff // FFN_CHUNK
    y = None
    for c in range(n_chunks):
        halves = []
        for part, off in enumerate((c * FFN_CHUNK, d_ff + c * FFN_CHUNK)):
            src = slice(off, off + FFN_CHUNK)
            us_ref = (us_g, us_v)[part]
            _stage_conv_input(_dot(hn, wup_ref[:, src]), us_ref, slice(None), halo_ref, src, taps)
            halves.append(_causal_conv(us_ref, slice(None), tm, cw_ref, src, cb_ref[:, src]))
        act = (_gelu_tanh(halves[0]) * halves[1]).astype(BF16)
        contrib = _dot(act, wdn_ref[c * FFN_CHUNK:(c + 1) * FFN_CHUNK, :])
        if c == n_chunks - 1:
            y = contrib if c == 0 else acc_ref[...] + contrib
        elif c == 0:
            acc_ref[...] = contrib
        else:
            acc_ref[...] += contrib
    _residual_from_interleaved(_rms(y, gpost_ref[...]), stage_ref, x_ref, out_ref)


def _ffn_scratch(tm, d, d_ff, taps):
    return ([pltpu.VMEM((taps - 1, 2 * d_ff), F32)] + _interleave_scratch(tm, d)
            + [pltpu.VMEM(((taps - 1) * SEGMENTS + tm, FFN_CHUNK), F32)] * 2
            + [pltpu.VMEM((tm, d), F32)])


def _ffn_operands(g_pre, g_post, w_up, conv_w, conv_b, w_down):
    d, d_ff = w_down.shape[1], w_down.shape[0]
    return [g_pre.reshape(1, d), g_post.reshape(1, d), w_up.astype(BF16), conv_w,
            conv_b.reshape(1, 2 * d_ff), w_down.astype(BF16)]


def _ffn_kernel(x_ref, gpre_ref, gpost_ref, wup_ref, cw_ref, cb_ref, wdn_ref, out_ref,
                halo_ref, *scratch):
    @pl.when(pl.program_id(1) == 0)
    def _():
        halo_ref[...] = jnp.zeros_like(halo_ref)

    _ffn_tile(x_ref.at[0], out_ref.at[0], gpre_ref, gpost_ref, wup_ref, cw_ref, cb_ref, wdn_ref,
              halo_ref, *scratch)


def _ffn_layer(x, g_pre, g_post, w_up, conv_w, conv_b, w_down):
    batch, seq, d = x.shape
    tm = min(ROW_TILE, seq)
    tile = pl.BlockSpec((1, tm, d), lambda b, i: (b, i, 0))
    ops = _ffn_operands(g_pre, g_post, w_up, conv_w, conv_b, w_down)
    return pl.pallas_call(
        _ffn_kernel,
        grid=(batch, seq // tm),
        in_specs=[tile] + [_resident(o.shape) for o in ops],
        out_specs=tile,
        out_shape=jax.ShapeDtypeStruct(x.shape, F32),
        scratch_shapes=_ffn_scratch(tm, d, w_down.shape[0], conv_w.shape[0]),
        compiler_params=_params("parallel", "arbitrary",
                                fuse_inputs=[False] + [o.dtype == BF16 for o in ops]),
        name="conv_ffn",
    )(x, *ops)


def kernel(x, norm_mix_pre, norm_mix_post, norm_ffn_pre, norm_ffn_post, rel_bias, attn_w_qkv, attn_w_o, lru_w_in, lru_conv_w, lru_conv_b, lru_ga_w, lru_ga_b, lru_gx_w, lru_gx_b, lru_lambda, lru_w_out, ffn_w_up, ffn_conv_w, ffn_conv_b, ffn_w_down):
    bias = _bias_tables(rel_bias)
    for layer in range(norm_mix_pre.shape[0]):
        j = layer // 2
        ffn = (norm_ffn_pre[layer], norm_ffn_post[layer], ffn_w_up[layer],
               ffn_conv_w[layer], ffn_conv_b[layer], ffn_w_down[layer])
        if layer % 2 == 0:
            x = _attention_layer(x, norm_mix_pre[layer], norm_mix_post[layer],
                                 attn_w_qkv[j], attn_w_o[j], bias)
        else:
            x = _lru_layer(x, norm_mix_pre[layer], norm_mix_post[layer], lru_w_in[j],
                           lru_conv_w[j], lru_conv_b[j], lru_ga_w[j], lru_ga_b[j],
                           lru_gx_w[j], lru_gx_b[j], lru_lambda[j], lru_w_out[j])
        x = _ffn_layer(x, *ffn)
    return x
```
